```python
import math
import jax, jax.numpy as jnp
from jax import lax
import numpy as np

D_MODEL = 1024
BATCH = 2
SEQ = 8192
DEPTH = 4

POOL_WIDTH = D_MODEL // 4
FOURIER_WIDTH = D_MODEL // 4
ATTN_WIDTH = D_MODEL // 2
POOL_WINDOWS = (2, 4, 8, 16)
N_POOL_GROUPS = len(POOL_WINDOWS)
POOL_GROUP_DIM = POOL_WIDTH // N_POOL_GROUPS
N_FOURIER_GROUPS = 4
FOURIER_GROUP_DIM = FOURIER_WIDTH // N_FOURIER_GROUPS
ATTN_HEADS = 4
ATTN_HEAD_DIM = ATTN_WIDTH // (2 * ATTN_HEADS)
ATTN_V_DIM = 2 * ATTN_HEAD_DIM
Q_BLOCK = 128
D_FF = 2816
IN_COLS = POOL_WIDTH + FOURIER_WIDTH + 3 * ATTN_WIDTH
NORM_EPS = 1e-6

kernel_name = "hybrid_pool_fourier_diffattn_macaron_encoder"


def alibi_slopes(n_heads):
    return jnp.array([2.0 ** (-8.0 * (i + 1) / n_heads) for i in range(n_heads)], dtype=jnp.float32)


def lambda_init_fn(layer_idx):
    return 0.8 - 0.6 * math.exp(-0.3 * layer_idx)


def rmsnorm(x, g):
    xf = x.astype(jnp.float32)
    y = xf * lax.rsqrt(jnp.mean(xf * xf, axis=-1, keepdims=True) + NORM_EPS)
    return (y * g.astype(jnp.float32)).astype(x.dtype)


def swiglu(x, w_gate, w_up, w_down):
    return (jax.nn.silu(x @ w_gate) * (x @ w_up)) @ w_down


def pool_mixer(a, pool_w, pool_scale):
    B, S, _ = a.shape
    af = a.reshape(B, S, N_POOL_GROUPS, POOL_GROUP_DIM).astype(jnp.float32)
    prefix = jnp.concatenate(
        [jnp.zeros((B, 1, N_POOL_GROUPS, POOL_GROUP_DIM), jnp.float32), jnp.cumsum(af, axis=1)], axis=1)
    t = jnp.arange(S)
    outs = []
    for g, w in enumerate(POOL_WINDOWS):
        left = w // 2
        right = w - 1 - left
        hi = jnp.minimum(t + right + 1, S)
        lo = jnp.maximum(t - left, 0)
        pg = prefix[:, :, g]
        win_sum = jnp.take(pg, hi, axis=1) - jnp.take(pg, lo, axis=1)
        cnt = (hi - lo).astype(jnp.float32)[None, :, None]
        outs.append(win_sum / cnt - af[:, :, g])
    m = jnp.stack(outs, axis=2).astype(a.dtype)
    y = jnp.einsum('bsgc,gcd->bsgd', m, pool_w).reshape(B, S, POOL_WIDTH)
    return y * pool_scale


def fourier_mixer(f, fourier_w):
    B, S, _ = f.shape
    f4 = f.reshape(B, S, N_FOURIER_GROUPS, FOURIER_GROUP_DIM).astype(jnp.float32)
    y = jnp.real(jnp.fft.fft2(f4, axes=(1, 3), norm='ortho'))
    y = y.astype(f.dtype).reshape(B, S, FOURIER_WIDTH)
    return y @ fourier_w


def diff_attention(q, k, v, lam, lam_init, head_norm):
    B, S = q.shape[0], q.shape[1]
    n_blk = S // Q_BLOCK
    scale = ATTN_HEAD_DIM ** -0.5
    slopes = alibi_slopes(ATTN_HEADS)
    kpos = jnp.arange(S, dtype=jnp.float32)
    qb = q.reshape(B, n_blk, Q_BLOCK, ATTN_HEADS, 2, ATTN_HEAD_DIM).transpose(1, 0, 2, 3, 4, 5)

    def one_block(args):
        q_blk, i = args
        qpos = (i * Q_BLOCK + jnp.arange(Q_BLOCK)).astype(jnp.float32)
        bias = -slopes[:, None, None] * jnp.abs(qpos[:, None] - kpos[None, :])
        s = jnp.einsum('bqhjd,bkhjd->bhjqk', q_blk, k).astype(jnp.float32) * scale + bias[None, :, None]
        p = jax.nn.softmax(s, axis=-1)
        attn = p[:, :, 0] - lam.astype(jnp.float32) * p[:, :, 1]
        return jnp.einsum('bhqk,bkhe->bqhe', attn.astype(v.dtype), v)

    o = lax.map(one_block, (qb, jnp.arange(n_blk)))
    o = o.transpose(1, 0, 2, 3, 4).reshape(B, S, ATTN_HEADS, ATTN_V_DIM)
    o = rmsnorm(o, head_norm.reshape(ATTN_HEADS, ATTN_V_DIM)) * (1.0 - lam_init)
    return o.reshape(B, S, ATTN_WIDTH)


def setup_inputs(seed: int = 0) -> dict:
    key = jax.random.key(seed)
    ks = jax.random.split(key, 24)
    f32 = jnp.float32

    def nrm(k, shape, fan_in):
        return jax.random.normal(k, shape, f32) * (fan_in ** -0.5)

    def gain(k, shape):
        return 1.0 + 0.05 * jax.random.normal(k, shape, f32)

    return {
        "x": jax.random.normal(ks[0], (BATCH, SEQ, D_MODEL), f32),
        "ffn1_norm": gain(ks[1], (DEPTH, D_MODEL)),
        "ffn1_w_gate": nrm(ks[2], (DEPTH, D_MODEL, D_FF), D_MODEL),
        "ffn1_w_up": nrm(ks[3], (DEPTH, D_MODEL, D_FF), D_MODEL),
        "ffn1_w_down": nrm(ks[4], (DEPTH, D_FF, D_MODEL), D_FF),
        "mix_norm": gain(ks[5], (DEPTH, D_MODEL)),
        "w_in": nrm(ks[6], (DEPTH, D_MODEL, IN_COLS), D_MODEL),
        "pool_w": nrm(ks[7], (DEPTH, N_POOL_GROUPS, POOL_GROUP_DIM, POOL_GROUP_DIM), POOL_GROUP_DIM),
        "pool_scale": gain(ks[8], (DEPTH, POOL_WIDTH)),
        "fourier_w": nrm(ks[9], (DEPTH, FOURIER_WIDTH, FOURIER_WIDTH), FOURIER_WIDTH),
        "lam_q1": 0.1 * jax.random.normal(ks[10], (DEPTH, ATTN_HEAD_DIM), f32),
        "lam_k1": 0.1 * jax.random.normal(ks[11], (DEPTH, ATTN_HEAD_DIM), f32),
        "lam_q2": 0.1 * jax.random.normal(ks[12], (DEPTH, ATTN_HEAD_DIM), f32),
        "lam_k2": 0.1 * jax.random.normal(ks[13], (DEPTH, ATTN_HEAD_DIM), f32),
        "attn_head_norm": gain(ks[14], (DEPTH, ATTN_WIDTH)),
        "w_out": nrm(ks[15], (DEPTH, D_MODEL, D_MODEL), D_MODEL),
        "ffn2_norm": gain(ks[16], (DEPTH, D_MODEL)),
        "ffn2_w_gate": nrm(ks[17], (DEPTH, D_MODEL, D_FF), D_MODEL),
        "ffn2_w_up": nrm(ks[18], (DEPTH, D_MODEL, D_FF), D_MODEL),
        "ffn2_w_down": nrm(ks[19], (DEPTH, D_FF, D_MODEL), D_FF),
        "final_norm": gain(ks[20], (D_MODEL,)),
    }


def reference(x, ffn1_norm, ffn1_w_gate, ffn1_w_up, ffn1_w_down, mix_norm, w_in, pool_w, pool_scale,
              fourier_w, lam_q1, lam_k1, lam_q2, lam_k2, attn_head_norm, w_out,
              ffn2_norm, ffn2_w_gate, ffn2_w_up, ffn2_w_down, final_norm):
    B, S, _ = x.shape
    c0 = POOL_WIDTH
    c1 = c0 + FOURIER_WIDTH
    c2 = c1 + ATTN_WIDTH
    c3 = c2 + ATTN_WIDTH
    for l in range(DEPTH):
        h = rmsnorm(x, ffn1_norm[l])
        x = x + 0.5 * swiglu(h, ffn1_w_gate[l], ffn1_w_up[l], ffn1_w_down[l])

        h = rmsnorm(x, mix_norm[l])
        p = h @ w_in[l]
        a = p[..., :c0]
        f = p[..., c0:c1]
        q = p[..., c1:c2].reshape(B, S, ATTN_HEADS, 2, ATTN_HEAD_DIM)
        k = p[..., c2:c3].reshape(B, S, ATTN_HEADS, 2, ATTN_HEAD_DIM)
        v = p[..., c3:].reshape(B, S, ATTN_HEADS, ATTN_V_DIM)

        lam_init = lambda_init_fn(l)
        lam = (jnp.exp(jnp.sum(lam_q1[l].astype(jnp.float32) * lam_k1[l].astype(jnp.float32)))
               - jnp.exp(jnp.sum(lam_q2[l].astype(jnp.float32) * lam_k2[l].astype(jnp.float32)))
               + lam_init)

        y_pool = pool_mixer(a, pool_w[l], pool_scale[l])
        y_four = fourier_mixer(f, fourier_w[l])
        y_attn = diff_attention(q, k, v, lam, lam_init, attn_head_norm[l]).astype(x.dtype)
        y = jnp.concatenate([y_pool, y_four, y_attn], axis=-1) @ w_out[l]
        x = x + y

        h = rmsnorm(x, ffn2_norm[l])
        x = x + 0.5 * swiglu(h, ffn2_w_gate[l], ffn2_w_up[l], ffn2_w_down[l])
    return rmsnorm(x, final_norm)
```

```python
import functools
import math

import numpy as np
import jax
import jax.numpy as jnp
from jax import lax
from jax.experimental import pallas as pl
from jax.experimental.pallas import tpu as pltpu

_F32 = jnp.float32
_BF16 = jnp.bfloat16

NORM_EPS = 1e-6
POOL_WIDTH = 256
FOURIER_WIDTH = 256
ATTN_WIDTH = 512
POOL_WINDOWS = (2, 4, 8, 16)
GROUP_DIM = 64
ATTN_HEADS = 4
HEAD_DIM = 64
V_DIM = 2 * HEAD_DIM
POOL_HALO = 8
DFT_L = 128
LANES = 128

_VMEM_LIMIT = 56 * 1024 * 1024


def _params(*semantics):
    return pltpu.CompilerParams(dimension_semantics=semantics, vmem_limit_bytes=_VMEM_LIMIT)


def _rms_scale(x, g):
    ms = jnp.mean(x * x, axis=-1, keepdims=True)
    return x * lax.rsqrt(ms + NORM_EPS) * g


def _split_hi_lo(x):
    hi = x.astype(_BF16)
    lo = (x - hi.astype(_F32)).astype(_BF16)
    return hi, lo


def _dot3(a_hi, a_lo, b_hi, b_lo):
    d = functools.partial(jnp.dot, preferred_element_type=_F32)
    return d(a_hi, b_hi) + (d(a_lo, b_hi) + d(a_hi, b_lo))


def _ffn_kernel(x_ref, g_ref, wg_ref, wu_ref, wd_ref, *rest, ff_chunk, final):
    if final:
        gf_ref, o_ref = rest
    else:
        (o_ref,) = rest
    x = x_ref[...]
    hn = _rms_scale(x, g_ref[...]).astype(_BF16)
    acc = jnp.zeros(x.shape, _F32)
    for c in range(wg_ref.shape[1] // ff_chunk):
        sl = slice(c * ff_chunk, (c + 1) * ff_chunk)
        gate = jnp.dot(hn, wg_ref[:, sl], preferred_element_type=_F32)
        up = jnp.dot(hn, wu_ref[:, sl], preferred_element_type=_F32)
        act = (gate / (1.0 + jnp.exp(-gate)) * up).astype(_BF16)
        acc = acc + jnp.dot(act, wd_ref[sl, :], preferred_element_type=_F32)
    y = x + 0.5 * acc
    if final:
        y = _rms_scale(y, gf_ref[...])
    o_ref[...] = y


def _ffn(x2d, g, wg, wu, wd, final_g=None, *, tm=512, ff_chunk=256):
    n, d = x2d.shape
    d_ff = wg.shape[1]
    const = lambda i: (0, 0)
    resident = lambda shape: pl.BlockSpec(shape, const, pipeline_mode=pl.Buffered(1))
    in_specs = [pl.BlockSpec((tm, d), lambda i: (i, 0)), pl.BlockSpec((1, d), const),
                resident((d, d_ff)), resident((d, d_ff)), resident((d_ff, d))]
    args = [x2d, g.reshape(1, d), wg, wu, wd]
    if final_g is not None:
        in_specs.append(pl.BlockSpec((1, d), const))
        args.append(final_g.reshape(1, d))
    return pl.pallas_call(
        functools.partial(_ffn_kernel, ff_chunk=ff_chunk, final=final_g is not None),
        grid=(n // tm,),
        in_specs=in_specs,
        out_specs=pl.BlockSpec((tm, d), lambda i: (i, 0)),
        out_shape=jax.ShapeDtypeStruct((n, d), _F32),
        compiler_params=_params("parallel"),
        name="ffn",
    )(*args)


def _mix_in_kernel(x_ref, g_ref, waf_ref, wqk_ref, wvt_ref, chi_ref, clo_ref,
                   a_ref, z_ref, q_ref, k_ref, vt_ref):
    hn = _rms_scale(x_ref[0], g_ref[...]).astype(_BF16)
    af = jnp.dot(hn, waf_ref[...], preferred_element_type=_F32)
    a_ref[0] = af[:, :POOL_WIDTH]
    f_hi, f_lo = _split_hi_lo(af[:, POOL_WIDTH:])
    z_ref[0] = _dot3(f_hi, f_lo, chi_ref[...], clo_ref[...])
    qk = jnp.dot(hn, wqk_ref[...], preferred_element_type=_F32)
    q_ref[0] = (qk[:, :ATTN_WIDTH] * (HEAD_DIM ** -0.5)).astype(_BF16)
    k_ref[0] = qk[:, ATTN_WIDTH:].astype(_BF16)
    vt = lax.dot_general(wvt_ref[...], hn, (((1,), (1,)), ((), ())), preferred_element_type=_F32)
    vt_ref[0] = vt.astype(_BF16)


def _mix_in(x, g, waf, wqk, wvt, chan_hi, chan_lo, *, tm=512):
    b, s, d = x.shape
    const = lambda bi, i: (0, 0)
    row = lambda bi, i: (bi, i, 0)
    full = lambda arr: pl.BlockSpec(arr.shape, const)
    return pl.pallas_call(
        _mix_in_kernel,
        grid=(b, s // tm),
        in_specs=[pl.BlockSpec((1, tm, d), row), pl.BlockSpec((1, d), const),
                  full(waf), full(wqk), full(wvt), full(chan_hi), full(chan_lo)],
        out_specs=[pl.BlockSpec((1, tm, POOL_WIDTH), row),
                   pl.BlockSpec((1, tm, 2 * FOURIER_WIDTH), row),
                   pl.BlockSpec((1, tm, ATTN_WIDTH), row),
                   pl.BlockSpec((1, tm, ATTN_WIDTH), row),
                   pl.BlockSpec((1, ATTN_WIDTH, tm), lambda bi, i: (bi, 0, i))],
        out_shape=[jax.ShapeDtypeStruct((b, s, POOL_WIDTH), _F32),
                   jax.ShapeDtypeStruct((b, s, 2 * FOURIER_WIDTH), _F32),
                   jax.ShapeDtypeStruct((b, s, ATTN_WIDTH), _BF16),
                   jax.ShapeDtypeStruct((b, s, ATTN_WIDTH), _BF16),
                   jax.ShapeDtypeStruct((b, ATTN_WIDTH, s), _BF16)],
        compiler_params=_params("parallel", "parallel"),
        name="mix_in",
    )(x, g.reshape(1, d), waf, wqk, wvt, chan_hi, chan_lo)


def _dft_tables(s):
    h, l = s // DFT_L, DFT_L
    two_pi = 2.0 * np.pi
    c = np.arange(GROUP_DIM)
    ang = two_pi * np.outer(c, c) / GROUP_DIM
    eye = np.eye(FOURIER_WIDTH // GROUP_DIM)
    chan = np.concatenate([np.kron(eye, np.cos(ang)), np.kron(eye, -np.sin(ang))], axis=1)
    u = np.arange(h)
    ang1 = two_pi * np.outer(u, u) / h
    stage1 = np.concatenate([np.cos(ang1), -np.sin(ang1)], axis=0)
    t = np.arange(l)
    angt = two_pi * np.outer(u, t) / s
    tw_cos, tw_sin = np.cos(angt), np.sin(angt)
    ang2 = two_pi * np.outer(t, t) / l
    scale = 1.0 / math.sqrt(s * GROUP_DIM)
    stage2 = np.concatenate([np.cos(ang2), np.sin(ang2)], axis=1) * scale
    as_f32 = lambda a: jnp.asarray(a, _F32)
    return (_split_hi_lo(as_f32(chan)), _split_hi_lo(as_f32(stage1)), _split_hi_lo(as_f32(stage2)),
            jnp.repeat(as_f32(tw_cos), LANES, axis=1), jnp.repeat(as_f32(tw_sin), LANES, axis=1))


def _dft_stage1_kernel(z_ref, mhi_ref, mlo_ref, twc_ref, tws_ref, o_ref, *, slabs):
    h = z_ref.shape[1]
    w = FOURIER_WIDTH
    z_hi, z_lo = _split_hi_lo(z_ref[0])
    p = _dot3(mhi_ref[...], mlo_ref[...], z_hi, z_lo)
    for j in range(slabs):
        re, im = slice(2 * j * w, (2 * j + 1) * w), slice((2 * j + 1) * w, (2 * j + 2) * w)
        ar = p[:h, re] - p[h:, im]
        ai = p[:h, im] + p[h:, re]
        tw = slice(j * LANES, (j + 1) * LANES)
        tc = jnp.concatenate([twc_ref[:, tw]] * (w // LANES), axis=1)
        ts = jnp.concatenate([tws_ref[:, tw]] * (w // LANES), axis=1)
        o_ref[0, :, re] = ar * tc + ai * ts
        o_ref[0, :, im] = ai * tc - ar * ts


def _dft_stage1(z, m_hi, m_lo, tw_cos, tw_sin, *, slabs=16):
    b, s, zw = z.shape
    h = s // DFT_L
    zv = z.reshape(b, h, DFT_L * zw)
    const = lambda bi, j: (0, 0)
    out = pl.pallas_call(
        functools.partial(_dft_stage1_kernel, slabs=slabs),
        grid=(b, DFT_L // slabs),
        in_specs=[pl.BlockSpec((1, h, slabs * zw), lambda bi, j: (bi, 0, j)),
                  pl.BlockSpec(m_hi.shape, const), pl.BlockSpec(m_lo.shape, const),
                  pl.BlockSpec((h, slabs * LANES), lambda bi, j: (0, j)),
                  pl.BlockSpec((h, slabs * LANES), lambda bi, j: (0, j))],
        out_specs=pl.BlockSpec((1, h, slabs * zw), lambda bi, j: (bi, 0, j)),
        out_shape=jax.ShapeDtypeStruct(zv.shape, _F32),
        compiler_params=_params("parallel", "parallel"),
        name="dft_stage1",
    )(zv, m_hi, m_lo, tw_cos, tw_sin)
    return out.reshape(b, h, DFT_L, zw)


def _dft_stage2_kernel(a_ref, mhi_ref, mlo_ref, fw_ref, o_ref, *, rows):
    w = FOURIER_WIDTH
    for r in range(rows):
        blk = a_ref[0, r]
        stacked = jnp.concatenate([blk[:, :w], blk[:, w:]], axis=0)
        s_hi, s_lo = _split_hi_lo(stacked)
        y = _dot3(mhi_ref[...], mlo_ref[...], s_hi, s_lo)
        yw = jnp.dot(y.astype(_BF16), fw_ref[...], preferred_element_type=_F32)
        o_ref[0, :, r * w:(r + 1) * w] = yw.astype(o_ref.dtype)


def _dft_stage2(a4, m_hi, m_lo, fourier_w, *, rows=8):
    b, h, l, zw = a4.shape
    w = FOURIER_WIDTH
    const = lambda bi, j: (0, 0)
    out = pl.pallas_call(
        functools.partial(_dft_stage2_kernel, rows=rows),
        grid=(b, h // rows),
        in_specs=[pl.BlockSpec((1, rows, l, zw), lambda bi, j: (bi, j, 0, 0)),
                  pl.BlockSpec(m_hi.shape, const), pl.BlockSpec(m_lo.shape, const),
                  pl.BlockSpec(fourier_w.shape, const)],
        out_specs=pl.BlockSpec((1, l, rows * w), lambda bi, j: (bi, 0, j)),
        out_shape=jax.ShapeDtypeStruct((b, l, h * w), _BF16),
        compiler_params=_params("parallel", "parallel"),
        name="dft_stage2",
    )(a4, m_hi, m_lo, fourier_w)
    return out.reshape(b, l * h, w)


def _attn_kernel(slope_ref, lam_init_ref, lam_ref, gain_ref, q_ref, k_ref, vt_ref, o_ref, acc_ref,
                 *, tk):
    tq = q_ref.shape[1]
    s = k_ref.shape[1]
    head = pl.program_id(1)
    q_start = pl.program_id(2) * tq
    neg_slope = -slope_ref[head]

    qh = q_ref[0]
    lane = lax.broadcasted_iota(jnp.int32, qh.shape, 1)
    zero = jnp.zeros_like(qh)
    q_halves = (jnp.where(lane < HEAD_DIM, qh, zero), jnp.where(lane >= HEAD_DIM, qh, zero))
    dist0 = (lax.broadcasted_iota(jnp.int32, (tk, tq), 0)
             - lax.broadcasted_iota(jnp.int32, (tk, tq), 1) - q_start)

    acc_ref[...] = jnp.zeros(acc_ref.shape, _F32)
    nt = (((1,), (1,)), ((), ()))

    def body(i, carry):
        k_start = pl.multiple_of(i * tk, tk)
        kt = k_ref[0, pl.ds(k_start, tk), :]
        vt = vt_ref[0, :, pl.ds(k_start, tk)]
        bias = jnp.abs(dist0 + k_start).astype(_F32) * neg_slope
        new = []
        for j in range(2):
            m_old, l_old = carry[j]
            sc = lax.dot_general(kt, q_halves[j], nt, preferred_element_type=_F32) + bias
            m_new = jnp.maximum(m_old, jnp.max(sc, axis=0, keepdims=True))
            p = jnp.exp(sc - m_new)
            alpha = jnp.exp(m_old - m_new)
            l_new = alpha * l_old + jnp.sum(p, axis=0, keepdims=True)
            pv = jnp.dot(vt, p.astype(_BF16), preferred_element_type=_F32)
            acc_ref[j] = alpha * acc_ref[j] + pv
            new.append((m_new, l_new))
        return tuple(new)

    init = (jnp.full((1, tq), -jnp.inf, _F32), jnp.zeros((1, tq), _F32))
    (_, l0), (_, l1) = lax.fori_loop(0, s // tk, body, (init, init))

    lam_init = lam_init_ref[0]
    lp = lam_ref[...]
    lam = (jnp.exp(jnp.sum(lp[0:1] * lp[1:2], axis=-1, keepdims=True))
           - jnp.exp(jnp.sum(lp[2:3] * lp[3:4], axis=-1, keepdims=True)) + lam_init)
    o_t = acc_ref[0] / l0 - lam * (acc_ref[1] / l1)
    o = o_t.T
    o = _rms_scale(o, gain_ref[...]) * (1.0 - lam_init)
    o_ref[0] = o.astype(o_ref.dtype)


def _diff_attention(q, k, vt, slopes, lam_init, lam_params, head_gain, *, tq=256, tk=512):
    b, s, _ = q.shape
    smem = pl.BlockSpec(memory_space=pltpu.SMEM)
    return pl.pallas_call(
        functools.partial(_attn_kernel, tk=tk),
        grid=(b, ATTN_HEADS, s // tq),
        in_specs=[smem, smem,
                  pl.BlockSpec(lam_params.shape, lambda bi, hi, qi: (0, 0)),
                  pl.BlockSpec((1, V_DIM), lambda bi, hi, qi: (0, hi)),
                  pl.BlockSpec((1, tq, V_DIM), lambda bi, hi, qi: (bi, qi, hi)),
                  pl.BlockSpec((1, s, V_DIM), lambda bi, hi, qi: (bi, 0, hi)),
                  pl.BlockSpec((1, V_DIM, s), lambda bi, hi, qi: (bi, hi, 0))],
        out_specs=pl.BlockSpec((1, tq, V_DIM), lambda bi, hi, qi: (bi, qi, hi)),
        out_shape=jax.ShapeDtypeStruct((b, s, ATTN_WIDTH), _BF16),
        scratch_shapes=[pltpu.VMEM((2, V_DIM, tq), _F32)],
        compiler_params=_params("parallel", "parallel", "arbitrary"),
        name="diff_attn",
    )(slopes, lam_init, lam_params, head_gain.reshape(1, ATTN_WIDTH), q, k, vt)


def _mix_out_kernel(x_ref, a_ref, a_prev_ref, a_next_ref, pw_ref, ps_ref, yf_ref, yo_ref, wo_ref,
                    o_ref, buf_ref):
    i = pl.program_id(1)
    tm = a_ref.shape[1]
    s_len = tm * pl.num_programs(1)
    halo = POOL_HALO
    buf_ref[0:halo] = jnp.where(i > 0, a_prev_ref[0], 0.0)
    buf_ref[halo:halo + tm] = a_ref[0]
    buf_ref[halo + tm:] = jnp.where(i < pl.num_programs(1) - 1, a_next_ref[0], 0.0)
    shifted = lambda d: buf_ref[halo + d:halo + d + tm]
    a = shifted(0)

    lane = lax.broadcasted_iota(jnp.int32, (1, POOL_WIDTH), 1)
    pick = lambda vals: functools.reduce(
        lambda acc, gv: jnp.where(lane >= gv[0] * GROUP_DIM, gv[1], acc), list(enumerate(vals))[1:], vals[0])
    sums, total, lo, hi = [], a, 0, 0
    for w in POOL_WINDOWS:
        left, right = w // 2, w - 1 - w // 2
        for d in list(range(-left, lo)) + list(range(hi + 1, right + 1)):
            total = total + shifted(d)
        lo, hi = -left, right
        sums.append(total)
    win = pick(sums)
    t = i * tm + lax.broadcasted_iota(jnp.int32, (tm, POOL_WIDTH), 0)
    left = pick([w // 2 for w in POOL_WINDOWS])
    right = pick([w - 1 - w // 2 for w in POOL_WINDOWS])
    cnt = jnp.minimum(t + right + 1, s_len) - jnp.maximum(t - left, 0)
    mixed = win / cnt.astype(_F32) - a
    y_pool = jnp.dot(mixed.astype(_BF16), pw_ref[...], preferred_element_type=_F32) * ps_ref[...]

    cat = jnp.concatenate([y_pool.astype(_BF16), yf_ref[0], yo_ref[0]], axis=1)
    o_ref[0] = x_ref[0] + jnp.dot(cat, wo_ref[...], preferred_element_type=_F32)


def _mix_out(x, a, pool_bd, pool_scale, y_four, y_attn, w_out, *, tm=512):
    b, s, d = x.shape
    hb = tm // POOL_HALO
    n_halo = s // POOL_HALO
    row = lambda bi, i: (bi, i, 0)
    const = lambda bi, i: (0, 0)
    return pl.pallas_call(
        _mix_out_kernel,
        grid=(b, s // tm),
        in_specs=[pl.BlockSpec((1, tm, d), row),
                  pl.BlockSpec((1, tm, POOL_WIDTH), row),
                  pl.BlockSpec((1, POOL_HALO, POOL_WIDTH), lambda bi, i: (bi, jnp.maximum(i * hb - 1, 0), 0)),
                  pl.BlockSpec((1, POOL_HALO, POOL_WIDTH),
                               lambda bi, i: (bi, jnp.minimum((i + 1) * hb, n_halo - 1), 0)),
                  pl.BlockSpec(pool_bd.shape, const), pl.BlockSpec((1, POOL_WIDTH), const),
                  pl.BlockSpec((1, tm, FOURIER_WIDTH), row),
                  pl.BlockSpec((1, tm, ATTN_WIDTH), row),
                  pl.BlockSpec(w_out.shape, const)],
        out_specs=pl.BlockSpec((1, tm, d), row),
        out_shape=jax.ShapeDtypeStruct((b, s, d), _F32),
        scratch_shapes=[pltpu.VMEM((tm + 2 * POOL_HALO, POOL_WIDTH), _F32)],
        compiler_params=_params("parallel", "parallel"),
        name="mix_out",
    )(x, a, a, a, pool_bd, pool_scale.reshape(1, POOL_WIDTH), y_four, y_attn, w_out)


def kernel(x, ffn1_norm, ffn1_w_gate, ffn1_w_up, ffn1_w_down, mix_norm, w_in, pool_w, pool_scale, fourier_w, lam_q1, lam_k1, lam_q2, lam_k2, attn_head_norm, w_out, ffn2_norm, ffn2_w_gate, ffn2_w_up, ffn2_w_down, final_norm):
    b, s, d = x.shape
    depth = w_in.shape[0]
    bf = lambda w: w.astype(_BF16)
    (chan_hi, chan_lo), (m1_hi, m1_lo), (m2_hi, m2_lo), tw_cos, tw_sin = _dft_tables(s)
    slopes = jnp.asarray([2.0 ** (-8.0 * (i + 1) / ATTN_HEADS) for i in range(ATTN_HEADS)], _F32)
    c_af, c_qk = POOL_WIDTH + FOURIER_WIDTH, POOL_WIDTH + FOURIER_WIDTH + 2 * ATTN_WIDTH

    for l in range(depth):
        x = _ffn(x.reshape(b * s, d), ffn1_norm[l], bf(ffn1_w_gate[l]), bf(ffn1_w_up[l]),
                 bf(ffn1_w_down[l])).reshape(b, s, d)

        w_l = bf(w_in[l])
        a, z, q, k, vt = _mix_in(x, mix_norm[l], w_l[:, :c_af], w_l[:, c_af:c_qk], w_l[:, c_qk:].T,
                                 chan_hi, chan_lo)
        y_four = _dft_stage2(_dft_stage1(z, m1_hi, m1_lo, tw_cos, tw_sin), m2_hi, m2_lo, bf(fourier_w[l]))
        lam_init = jnp.full((1,), 0.8 - 0.6 * math.exp(-0.3 * l), _F32)
        lam_params = jnp.stack([lam_q1[l], lam_k1[l], lam_q2[l], lam_k2[l]]).astype(_F32)
        y_attn = _diff_attention(q, k, vt, slopes, lam_init, lam_params, attn_head_norm[l])
        pool_bd = jax.scipy.linalg.block_diag(*[pool_w[l, g] for g in range(len(POOL_WINDOWS))])
        x = _mix_out(x, a, bf(pool_bd), pool_scale[l], y_four, y_attn, bf(w_out[l]))

        x = _ffn(x.reshape(b * s, d), ffn2_norm[l], bf(ffn2_w_gate[l]), bf(ffn2_w_up[l]),
                 bf(ffn2_w_down[l]), final_norm if l == depth - 1 else None).reshape(b, s, d)
    return x
```

```python
import functools
import math

import numpy as np
import jax
import jax.numpy as jnp
from jax import lax
from jax.experimental import pallas as pl
from jax.experimental.pallas import tpu as pltpu

_F32 = jnp.float32
_BF16 = jnp.bfloat16

NORM_EPS = 1e-6
POOL_WIDTH = 256
FOURIER_WIDTH = 256
ATTN_WIDTH = 512
POOL_WINDOWS = (2, 4, 8, 16)
GROUP_DIM = 64
ATTN_HEADS = 4
HEAD_DIM = 64
V_DIM = 2 * HEAD_DIM
POOL_HALO = 8
DFT_L = 128
LANES = 128
LOG2_E = math.log2(math.e)

_VMEM_LIMIT = 56 * 1024 * 1024


def _params(*semantics, flags=None):
    return pltpu.CompilerParams(dimension_semantics=semantics, vmem_limit_bytes=_VMEM_LIMIT, flags=flags)


def _rms_scale(x, g):
    ms = jnp.mean(x * x, axis=-1, keepdims=True)
    return x * lax.rsqrt(ms + NORM_EPS) * g


def _split_hi_lo(x):
    hi = x.astype(_BF16)
    lo = (x - hi.astype(_F32)).astype(_BF16)
    return hi, lo


def _dot3(a_hi, a_lo, b_hi, b_lo):
    d = functools.partial(jnp.dot, preferred_element_type=_F32)
    return d(a_hi, b_hi) + (d(a_lo, b_hi) + d(a_hi, b_lo))


def _ffn_kernel(x_ref, g_ref, wg_ref, wu_ref, wd_ref, *rest, ff_chunk, final):
    if final:
        gf_ref, o_ref = rest
    else:
        (o_ref,) = rest
    x = x_ref[...]
    hn = _rms_scale(x, g_ref[...]).astype(_BF16)
    acc = jnp.zeros(x.shape, _F32)
    for c in range(wg_ref.shape[1] // ff_chunk):
        sl = slice(c * ff_chunk, (c + 1) * ff_chunk)
        gate = jnp.dot(hn, wg_ref[:, sl], preferred_element_type=_F32)
        up = jnp.dot(hn, wu_ref[:, sl], preferred_element_type=_F32)
        act = (gate / (1.0 + jnp.exp(-gate)) * up).astype(_BF16)
        acc = acc + jnp.dot(act, wd_ref[sl, :], preferred_element_type=_F32)
    y = x + 0.5 * acc
    if final:
        y = _rms_scale(y, gf_ref[...])
    o_ref[...] = y


def _ffn(x2d, g, wg, wu, wd, final_g=None, *, tm=512, ff_chunk=256):
    n, d = x2d.shape
    d_ff = wg.shape[1]
    const = lambda i: (0, 0)
    resident = lambda shape: pl.BlockSpec(shape, const, pipeline_mode=pl.Buffered(1))
    in_specs = [pl.BlockSpec((tm, d), lambda i: (i, 0)), pl.BlockSpec((1, d), const),
                resident((d, d_ff)), resident((d, d_ff)), resident((d_ff, d))]
    args = [x2d, g.reshape(1, d), wg, wu, wd]
    if final_g is not None:
        in_specs.append(pl.BlockSpec((1, d), const))
        args.append(final_g.reshape(1, d))
    return pl.pallas_call(
        functools.partial(_ffn_kernel, ff_chunk=ff_chunk, final=final_g is not None),
        grid=(n // tm,),
        in_specs=in_specs,
        out_specs=pl.BlockSpec((tm, d), lambda i: (i, 0)),
        out_shape=jax.ShapeDtypeStruct((n, d), _F32),
        compiler_params=_params("parallel"),
        name="ffn",
    )(*args)


def _mix_in_kernel(x_ref, g_ref, waf_ref, wqk_ref, wvt_ref, chi_ref, clo_ref,
                   a_ref, z_ref, q_ref, k_ref, vt_ref):
    hn = _rms_scale(x_ref[0], g_ref[...]).astype(_BF16)
    af = jnp.dot(hn, waf_ref[...], preferred_element_type=_F32)
    a_ref[0] = af[:, :POOL_WIDTH]
    f_hi, f_lo = _split_hi_lo(af[:, POOL_WIDTH:])
    z_ref[0] = _dot3(f_hi, f_lo, chi_ref[...], clo_ref[...])
    qk = jnp.dot(hn, wqk_ref[...], preferred_element_type=_F32)
    q_ref[0] = (qk[:, :ATTN_WIDTH] * (HEAD_DIM ** -0.5 * LOG2_E)).astype(_BF16)
    k_ref[0] = qk[:, ATTN_WIDTH:].astype(_BF16)
    vt = lax.dot_general(wvt_ref[...], hn, (((1,), (1,)), ((), ())), preferred_element_type=_F32)
    vt_ref[0] = vt.astype(_BF16)


def _mix_in(x, g, waf, wqk, wvt, chan_hi, chan_lo, *, tm=512):
    b, s, d = x.shape
    const = lambda bi, i: (0, 0)
    row = lambda bi, i: (bi, i, 0)
    full = lambda arr: pl.BlockSpec(arr.shape, const)
    return pl.pallas_call(
        _mix_in_kernel,
        grid=(b, s // tm),
        in_specs=[pl.BlockSpec((1, tm, d), row), pl.BlockSpec((1, d), const),
                  full(waf), full(wqk), full(wvt), full(chan_hi), full(chan_lo)],
        out_specs=[pl.BlockSpec((1, tm, POOL_WIDTH), row),
                   pl.BlockSpec((1, tm, 2 * FOURIER_WIDTH), row),
                   pl.BlockSpec((1, tm, ATTN_WIDTH), row),
                   pl.BlockSpec((1, tm, ATTN_WIDTH), row),
                   pl.BlockSpec((1, ATTN_WIDTH, tm), lambda bi, i: (bi, 0, i))],
        out_shape=[jax.ShapeDtypeStruct((b, s, POOL_WIDTH), _F32),
                   jax.ShapeDtypeStruct((b, s, 2 * FOURIER_WIDTH), _F32),
                   jax.ShapeDtypeStruct((b, s, ATTN_WIDTH), _BF16),
                   jax.ShapeDtypeStruct((b, s, ATTN_WIDTH), _BF16),
                   jax.ShapeDtypeStruct((b, ATTN_WIDTH, s), _BF16)],
        compiler_params=_params("parallel", "parallel"),
        name="mix_in",
    )(x, g.reshape(1, d), waf, wqk, wvt, chan_hi, chan_lo)


def _dft_tables(s):
    h, l = s // DFT_L, DFT_L
    two_pi = 2.0 * np.pi
    c = np.arange(GROUP_DIM)
    ang = two_pi * np.outer(c, c) / GROUP_DIM
    eye = np.eye(FOURIER_WIDTH // GROUP_DIM)
    chan = np.concatenate([np.kron(eye, np.cos(ang)), np.kron(eye, -np.sin(ang))], axis=1)
    u = np.arange(h)
    ang1 = two_pi * np.outer(u, u) / h
    stage1 = np.concatenate([np.cos(ang1), -np.sin(ang1)], axis=0)
    t = np.arange(l)
    angt = two_pi * np.outer(u, t) / s
    tw_cos, tw_sin = np.cos(angt), np.sin(angt)
    ang2 = two_pi * np.outer(t, t) / l
    scale = 1.0 / math.sqrt(s * GROUP_DIM)
    stage2 = np.concatenate([np.cos(ang2), np.sin(ang2)], axis=1) * scale
    as_f32 = lambda a: jnp.asarray(a, _F32)
    return (_split_hi_lo(as_f32(chan)), _split_hi_lo(as_f32(stage1)), _split_hi_lo(as_f32(stage2)),
            jnp.repeat(as_f32(tw_cos), LANES, axis=1), jnp.repeat(as_f32(tw_sin), LANES, axis=1))


def _dft_stage1_kernel(z_ref, mhi_ref, mlo_ref, twc_ref, tws_ref, o_ref, *, slabs):
    h = z_ref.shape[1]
    w = FOURIER_WIDTH
    z_hi, z_lo = _split_hi_lo(z_ref[0])
    p = _dot3(mhi_ref[...], mlo_ref[...], z_hi, z_lo)
    for j in range(slabs):
        re, im = slice(2 * j * w, (2 * j + 1) * w), slice((2 * j + 1) * w, (2 * j + 2) * w)
        ar = p[:h, re] - p[h:, im]
        ai = p[:h, im] + p[h:, re]
        tw = slice(j * LANES, (j + 1) * LANES)
        tc = jnp.concatenate([twc_ref[:, tw]] * (w // LANES), axis=1)
        ts = jnp.concatenate([tws_ref[:, tw]] * (w // LANES), axis=1)
        o_ref[0, :, re] = ar * tc + ai * ts
        o_ref[0, :, im] = ai * tc - ar * ts


def _dft_stage1(z, m_hi, m_lo, tw_cos, tw_sin, *, slabs=16):
    b, s, zw = z.shape
    h = s // DFT_L
    zv = z.reshape(b, h, DFT_L * zw)
    const = lambda bi, j: (0, 0)
    out = pl.pallas_call(
        functools.partial(_dft_stage1_kernel, slabs=slabs),
        grid=(b, DFT_L // slabs),
        in_specs=[pl.BlockSpec((1, h, slabs * zw), lambda bi, j: (bi, 0, j)),
                  pl.BlockSpec(m_hi.shape, const), pl.BlockSpec(m_lo.shape, const),
                  pl.BlockSpec((h, slabs * LANES), lambda bi, j: (0, j)),
                  pl.BlockSpec((h, slabs * LANES), lambda bi, j: (0, j))],
        out_specs=pl.BlockSpec((1, h, slabs * zw), lambda bi, j: (bi, 0, j)),
        out_shape=jax.ShapeDtypeStruct(zv.shape, _F32),
        compiler_params=_params("parallel", "parallel"),
        name="dft_stage1",
    )(zv, m_hi, m_lo, tw_cos, tw_sin)
    return out.reshape(b, h, DFT_L, zw)


def _dft_stage2_kernel(a_ref, mhi_ref, mlo_ref, fw_ref, o_ref, *, rows):
    w = FOURIER_WIDTH
    for r in range(rows):
        blk = a_ref[0, r]
        stacked = jnp.concatenate([blk[:, :w], blk[:, w:]], axis=0)
        s_hi, s_lo = _split_hi_lo(stacked)
        y = _dot3(mhi_ref[...], mlo_ref[...], s_hi, s_lo)
        yw = jnp.dot(y.astype(_BF16), fw_ref[...], preferred_element_type=_F32)
        o_ref[0, :, r * w:(r + 1) * w] = yw.astype(o_ref.dtype)


def _dft_stage2(a4, m_hi, m_lo, fourier_w, *, rows=8):
    b, h, l, zw = a4.shape
    w = FOURIER_WIDTH
    const = lambda bi, j: (0, 0)
    out = pl.pallas_call(
        functools.partial(_dft_stage2_kernel, rows=rows),
        grid=(b, h // rows),
        in_specs=[pl.BlockSpec((1, rows, l, zw), lambda bi, j: (bi, j, 0, 0)),
                  pl.BlockSpec(m_hi.shape, const), pl.BlockSpec(m_lo.shape, const),
                  pl.BlockSpec(fourier_w.shape, const)],
        out_specs=pl.BlockSpec((1, l, rows * w), lambda bi, j: (bi, 0, j)),
        out_shape=jax.ShapeDtypeStruct((b, l, h * w), _BF16),
        compiler_params=_params("parallel", "parallel"),
        name="dft_stage2",
    )(a4, m_hi, m_lo, fourier_w)
    return out.reshape(b, l * h, w)


def _bias_feature_tables(tile):
    pos = np.arange(tile)
    hi, lo = (pos // LANES) * LANES, pos % LANES
    qf = np.zeros((tile, LANES), np.float32)
    kf = np.zeros((tile, LANES), np.float32)
    rest = LOG2_E
    for p in range(3):
        part = float(np.float32(rest).astype(jnp.bfloat16).astype(np.float32))
        rest -= part
        qf[:, 2 * p], qf[:, 2 * p + 1] = hi, lo
        kf[:, 2 * p], kf[:, 2 * p + 1] = part, part
        qf[:, 6 + 2 * p], qf[:, 7 + 2 * p] = -part, -part
        kf[:, 6 + 2 * p], kf[:, 7 + 2 * p] = hi, lo
    return jnp.asarray(qf, _BF16), jnp.asarray(np.stack([-kf, 0.0 * kf, kf]), _BF16)


def _attn_kernel(slope_ref, rate_ref, lam_init_ref, lam_ref, gain_ref, qf_ref, kf_ref, q_ref, k_ref, vt_ref,
                 o_ref, w_ref, acc_ref, frame_ref, l_ref, p_a, p_b):
    t = q_ref.shape[1]
    n = k_ref.shape[1] // t
    head = pl.program_id(1)
    qi = pl.program_id(2)
    neg_c = -rate_ref[head]
    nt = (((1,), (1,)), ((), ()))

    qh = q_ref[0]
    lane = lax.broadcasted_iota(jnp.int32, qh.shape, 1)
    zero = jnp.zeros_like(qh)
    halves = jnp.concatenate([jnp.where(lane < HEAD_DIM, qh, zero), jnp.where(lane >= HEAD_DIM, qh, zero)],
                             axis=0)
    qf = (qf_ref[...].astype(_F32) * slope_ref[head]).astype(_BF16)
    w_ref[...] = jnp.concatenate([halves, jnp.concatenate([qf, qf], axis=0)], axis=1)

    def qk(tile_idx, side):
        k0 = pl.multiple_of(tile_idx * t, t)
        lhs = jnp.concatenate([k_ref[0, pl.ds(k0, t), :], kf_ref[side]], axis=1)
        return lax.dot_general(lhs, w_ref[...], nt, preferred_element_type=_F32)

    def explicit_bias(tile_delta):
        dist = jnp.abs(lax.broadcasted_iota(jnp.int32, (t, t), 0) - lax.broadcasted_iota(jnp.int32, (t, t), 1)
                       + tile_delta * t)
        bias = dist.astype(_F32) * neg_c
        return jnp.concatenate([bias, bias], axis=1)

    def pv(tile_idx, p_ref):
        k0 = pl.multiple_of(tile_idx * t, t)
        return jnp.dot(vt_ref[0, :, pl.ds(k0, t)], p_ref[...], preferred_element_type=_F32)

    def finish(o_all):
        lam_init = lam_init_ref[0]
        lp = lam_ref[...]
        lam = (jnp.exp(jnp.sum(lp[0:1] * lp[1:2], axis=-1, keepdims=True))
               - jnp.exp(jnp.sum(lp[2:3] * lp[3:4], axis=-1, keepdims=True)) + lam_init)
        o = (o_all[:, :t] - lam * o_all[:, t:]).T
        o = _rms_scale(o, gain_ref[...]) * (1.0 - lam_init)
        o_ref[0] = o.astype(o_ref.dtype)

    s_diag = qk(qi, 1) + explicit_bias(0)
    frame = jnp.max(s_diag, axis=0, keepdims=True)
    p_diag = jnp.exp2(s_diag - frame)
    frame_ref[...] = frame
    l_ref[...] = jnp.sum(p_diag, axis=0, keepdims=True)
    p_b[...] = p_diag.astype(_BF16)
    acc_ref[...] = jnp.zeros(acc_ref.shape, _F32)

    def off_diag_tile(j):
        return j + (qi <= j).astype(jnp.int32)

    def stage(j, p_cur, p_prev):
        i = off_diag_tile(j)
        off = jnp.abs(jnp.full((1, 2 * t), (i - qi) * t, jnp.int32)).astype(_F32) * neg_c
        p = jnp.exp2(qk(i, jnp.where(i < qi, 0, 2)) - (frame_ref[...] - off))
        l_ref[...] += jnp.sum(p, axis=0, keepdims=True)
        p_cur[...] = p.astype(_BF16)
        acc_ref[...] += pv(qi if j == 0 else off_diag_tile(j - 1), p_prev)

    for j in range(n - 1):
        stage(j, *((p_a, p_b) if j % 2 == 0 else (p_b, p_a)))
    last = n - 2
    o_all = (acc_ref[...] + pv(off_diag_tile(last), p_a if last % 2 == 0 else p_b)) / l_ref[...]
    finish(o_all)

    @pl.when(jnp.logical_not(jnp.sum(o_all * 0.0) == 0.0))
    def _():
        acc_ref[...] = jnp.zeros(acc_ref.shape, _F32)

        def safe_body(i, carry):
            m, l = carry
            sc = qk(i, 1) + explicit_bias(i - qi)
            m_new = jnp.maximum(m, jnp.max(sc, axis=0, keepdims=True))
            alpha = jnp.exp2(m - m_new)
            p = jnp.exp2(sc - m_new)
            p_a[...] = p.astype(_BF16)
            acc_ref[...] = alpha * acc_ref[...] + pv(i, p_a)
            return m_new, alpha * l + jnp.sum(p, axis=0, keepdims=True)

        init = (jnp.full((1, 2 * t), -jnp.inf, _F32), jnp.zeros((1, 2 * t), _F32))
        _, l_safe = lax.fori_loop(0, n, safe_body, init)
        finish(acc_ref[...] / l_safe)


def _diff_attention(q, k, vt, slopes, lam_init, lam_params, head_gain, *, tile=512):
    b, s, _ = q.shape
    assert s // tile >= 2
    qf, kf = _bias_feature_tables(tile)
    smem = pl.BlockSpec(memory_space=pltpu.SMEM)
    const = lambda shape: pl.BlockSpec(shape, lambda bi, hi, qi: (0,) * len(shape))
    return pl.pallas_call(
        _attn_kernel,
        grid=(b, ATTN_HEADS, s // tile),
        in_specs=[smem, smem, smem, const(lam_params.shape),
                  pl.BlockSpec((1, V_DIM), lambda bi, hi, qi: (0, hi)),
                  const(qf.shape), const(kf.shape),
                  pl.BlockSpec((1, tile, V_DIM), lambda bi, hi, qi: (bi, qi, hi)),
                  pl.BlockSpec((1, s, V_DIM), lambda bi, hi, qi: (bi, 0, hi)),
                  pl.BlockSpec((1, V_DIM, s), lambda bi, hi, qi: (bi, hi, 0))],
        out_specs=pl.BlockSpec((1, tile, V_DIM), lambda bi, hi, qi: (bi, qi, hi)),
        out_shape=jax.ShapeDtypeStruct((b, s, ATTN_WIDTH), _BF16),
        scratch_shapes=[pltpu.VMEM((2 * tile, 2 * LANES), _BF16),
                        pltpu.VMEM((V_DIM, 2 * tile), _F32),
                        pltpu.VMEM((1, 2 * tile), _F32),
                        pltpu.VMEM((1, 2 * tile), _F32),
                        pltpu.VMEM((tile, 2 * tile), _BF16), pltpu.VMEM((tile, 2 * tile), _BF16)],
        compiler_params=_params("parallel", "parallel", "arbitrary"),
        name="diff_attn",
    )(slopes, slopes * LOG2_E, lam_init, lam_params, head_gain.reshape(1, ATTN_WIDTH), qf, kf, q, k, vt)


def _mix_out_kernel(x_ref, a_ref, a_prev_ref, a_next_ref, pw_ref, ps_ref, yf_ref, yo_ref, wo_ref,
                    o_ref, buf_ref):
    i = pl.program_id(1)
    tm = a_ref.shape[1]
    s_len = tm * pl.num_programs(1)
    halo = POOL_HALO
    buf_ref[0:halo] = jnp.where(i > 0, a_prev_ref[0], 0.0)
    buf_ref[halo:halo + tm] = a_ref[0]
    buf_ref[halo + tm:] = jnp.where(i < pl.num_programs(1) - 1, a_next_ref[0], 0.0)
    shifted = lambda d: buf_ref[halo + d:halo + d + tm]
    a = shifted(0)

    lane = lax.broadcasted_iota(jnp.int32, (1, POOL_WIDTH), 1)
    pick = lambda vals: functools.reduce(
        lambda acc, gv: jnp.where(lane >= gv[0] * GROUP_DIM, gv[1], acc), list(enumerate(vals))[1:], vals[0])
    sums, total, lo, hi = [], a, 0, 0
    for w in POOL_WINDOWS:
        left, right = w // 2, w - 1 - w // 2
        for d in list(range(-left, lo)) + list(range(hi + 1, right + 1)):
            total = total + shifted(d)
        lo, hi = -left, right
        sums.append(total)
    win = pick(sums)
    t = i * tm + lax.broadcasted_iota(jnp.int32, (tm, POOL_WIDTH), 0)
    left = pick([w // 2 for w in POOL_WINDOWS])
    right = pick([w - 1 - w // 2 for w in POOL_WINDOWS])
    cnt = jnp.minimum(t + right + 1, s_len) - jnp.maximum(t - left, 0)
    mixed = win / cnt.astype(_F32) - a
    y_pool = jnp.dot(mixed.astype(_BF16), pw_ref[...], preferred_element_type=_F32) * ps_ref[...]

    cat = jnp.concatenate([y_pool.astype(_BF16), yf_ref[0], yo_ref[0]], axis=1)
    o_ref[0] = x_ref[0] + jnp.dot(cat, wo_ref[...], preferred_element_type=_F32)


def _mix_out(x, a, pool_bd, pool_scale, y_four, y_attn, w_out, *, tm=512):
    b, s, d = x.shape
    hb = tm // POOL_HALO
    n_halo = s // POOL_HALO
    row = lambda bi, i: (bi, i, 0)
    const = lambda bi, i: (0, 0)
    return pl.pallas_call(
        _mix_out_kernel,
        grid=(b, s // tm),
        in_specs=[pl.BlockSpec((1, tm, d), row),
                  pl.BlockSpec((1, tm, POOL_WIDTH), row),
                  pl.BlockSpec((1, POOL_HALO, POOL_WIDTH), lambda bi, i: (bi, jnp.maximum(i * hb - 1, 0), 0)),
                  pl.BlockSpec((1, POOL_HALO, POOL_WIDTH),
                               lambda bi, i: (bi, jnp.minimum((i + 1) * hb, n_halo - 1), 0)),
                  pl.BlockSpec(pool_bd.shape, const), pl.BlockSpec((1, POOL_WIDTH), const),
                  pl.BlockSpec((1, tm, FOURIER_WIDTH), row),
                  pl.BlockSpec((1, tm, ATTN_WIDTH), row),
                  pl.BlockSpec(w_out.shape, const)],
        out_specs=pl.BlockSpec((1, tm, d), row),
        out_shape=jax.ShapeDtypeStruct((b, s, d), _F32),
        scratch_shapes=[pltpu.VMEM((tm + 2 * POOL_HALO, POOL_WIDTH), _F32)],
        compiler_params=_params("parallel", "parallel"),
        name="mix_out",
    )(x, a, a, a, pool_bd, pool_scale.reshape(1, POOL_WIDTH), y_four, y_attn, w_out)


def kernel(x, ffn1_norm, ffn1_w_gate, ffn1_w_up, ffn1_w_down, mix_norm, w_in, pool_w, pool_scale, fourier_w, lam_q1, lam_k1, lam_q2, lam_k2, attn_head_norm, w_out, ffn2_norm, ffn2_w_gate, ffn2_w_up, ffn2_w_down, final_norm):
    b, s, d = x.shape
    depth = w_in.shape[0]
    bf = lambda w: w.astype(_BF16)
    (chan_hi, chan_lo), (m1_hi, m1_lo), (m2_hi, m2_lo), tw_cos, tw_sin = _dft_tables(s)
    alibi = [2.0 ** (-8.0 * (i + 1) / ATTN_HEADS) for i in range(ATTN_HEADS)]
    assert all(math.frexp(v)[0] == 0.5 for v in alibi), "position features rely on power-of-two slopes"
    slopes = jnp.asarray(alibi, _F32)
    c_af, c_qk = POOL_WIDTH + FOURIER_WIDTH, POOL_WIDTH + FOURIER_WIDTH + 2 * ATTN_WIDTH

    for l in range(depth):
        x = _ffn(x.reshape(b * s, d), ffn1_norm[l], bf(ffn1_w_gate[l]), bf(ffn1_w_up[l]),
                 bf(ffn1_w_down[l])).reshape(b, s, d)

        w_l = bf(w_in[l])
        a, z, q, k, vt = _mix_in(x, mix_norm[l], w_l[:, :c_af], w_l[:, c_af:c_qk], w_l[:, c_qk:].T,
                                 chan_hi, chan_lo)
        y_four = _dft_stage2(_dft_stage1(z, m1_hi, m1_lo, tw_cos, tw_sin), m2_hi, m2_lo, bf(fourier_w[l]))
        lam_init = jnp.full((1,), 0.8 - 0.6 * math.exp(-0.3 * l), _F32)
        lam_params = jnp.stack([lam_q1[l], lam_k1[l], lam_q2[l], lam_k2[l]]).astype(_F32)
        y_attn = _diff_attention(q, k, vt, slopes, lam_init, lam_params, attn_head_norm[l])
        pool_bd = jax.scipy.linalg.block_diag(*[pool_w[l, g] for g in range(len(POOL_WINDOWS))])
        x = _mix_out(x, a, bf(pool_bd), pool_scale[l], y_four, y_attn, bf(w_out[l]))

        x = _ffn(x.reshape(b * s, d), ffn2_norm[l], bf(ffn2_w_gate[l]), bf(ffn2_w_up[l]),
                 bf(ffn2_w_down[l]), final_norm if l == depth - 1 else None).reshape(b, s, d)
    return x
```

```python
import functools
import math

import numpy as np
import jax
import jax.numpy as jnp
from jax import lax
from jax.experimental import pallas as pl
from jax.experimental.pallas import tpu as pltpu

_F32 = jnp.float32
_BF16 = jnp.bfloat16

NORM_EPS = 1e-6
POOL_WIDTH = 256
FOURIER_WIDTH = 256
ATTN_WIDTH = 512
POOL_WINDOWS = (2, 4, 8, 16)
GROUP_DIM = 64
ATTN_HEADS = 4
HEAD_DIM = 64
V_DIM = 2 * HEAD_DIM
POOL_HALO = 8
DFT_L = 128
LANES = 128
LOG2_E = math.log2(math.e)
FLUSH_EXP = 127.0
FRAME_SLACK = 1.0 + 2.0 ** -5
NO_TILE = 1e30
MIN_DENOMINATOR = 2.0 ** -90

_VMEM_LIMIT = 56 * 1024 * 1024


def _params(*semantics, flags=None):
    return pltpu.CompilerParams(dimension_semantics=semantics, vmem_limit_bytes=_VMEM_LIMIT, flags=flags)


def _rms_scale(x, g):
    ms = jnp.mean(x * x, axis=-1, keepdims=True)
    return x * lax.rsqrt(ms + NORM_EPS) * g


def _split_hi_lo(x):
    hi = x.astype(_BF16)
    lo = (x - hi.astype(_F32)).astype(_BF16)
    return hi, lo


def _dot3(a_hi, a_lo, b_hi, b_lo):
    d = functools.partial(jnp.dot, preferred_element_type=_F32)
    return d(a_hi, b_hi) + (d(a_lo, b_hi) + d(a_hi, b_lo))


def _ffn_kernel(x_ref, g_ref, wg_ref, wu_ref, wd_ref, *rest, ff_chunk, final):
    if final:
        gf_ref, o_ref = rest
    else:
        (o_ref,) = rest
    x = x_ref[...]
    hn = _rms_scale(x, g_ref[...]).astype(_BF16)
    acc = jnp.zeros(x.shape, _F32)
    for c in range(wg_ref.shape[1] // ff_chunk):
        sl = slice(c * ff_chunk, (c + 1) * ff_chunk)
        gate = jnp.dot(hn, wg_ref[:, sl], preferred_element_type=_F32)
        up = jnp.dot(hn, wu_ref[:, sl], preferred_element_type=_F32)
        act = (gate / (1.0 + jnp.exp(-gate)) * up).astype(_BF16)
        acc = acc + jnp.dot(act, wd_ref[sl, :], preferred_element_type=_F32)
    y = x + 0.5 * acc
    if final:
        y = _rms_scale(y, gf_ref[...])
    o_ref[...] = y


def _ffn(x2d, g, wg, wu, wd, layer, final_g=None, *, tm=512, ff_chunk=256):
    n, d = x2d.shape
    d_ff = wg.shape[2]
    const = lambda i: (0, 0)
    resident = lambda shape: pl.BlockSpec((None,) + shape, lambda i: (layer, 0, 0), pipeline_mode=pl.Buffered(1))
    in_specs = [pl.BlockSpec((tm, d), lambda i: (i, 0)), pl.BlockSpec((1, d), const),
                resident((d, d_ff)), resident((d, d_ff)), resident((d_ff, d))]
    args = [x2d, g.reshape(1, d), wg, wu, wd]
    if final_g is not None:
        in_specs.append(pl.BlockSpec((1, d), const))
        args.append(final_g.reshape(1, d))
    return pl.pallas_call(
        functools.partial(_ffn_kernel, ff_chunk=ff_chunk, final=final_g is not None),
        grid=(n // tm,),
        in_specs=in_specs,
        out_specs=pl.BlockSpec((tm, d), lambda i: (i, 0)),
        out_shape=jax.ShapeDtypeStruct((n, d), _F32),
        compiler_params=_params("parallel"),
        name="ffn",
    )(*args)


def _mix_in_kernel(x_ref, g_ref, waf_ref, wq_ref, wk_ref, wvt_ref, chi_ref, clo_ref,
                   a_ref, z_ref, q_ref, k_ref, vt_ref):
    hn = _rms_scale(x_ref[0], g_ref[...]).astype(_BF16)
    af = jnp.dot(hn, waf_ref[...], preferred_element_type=_F32)
    a_ref[0] = af[:, :POOL_WIDTH]
    f_hi, f_lo = _split_hi_lo(af[:, POOL_WIDTH:])
    z_ref[0] = _dot3(f_hi, f_lo, chi_ref[...], clo_ref[...])
    q = jnp.dot(hn, wq_ref[...], preferred_element_type=_F32)
    q_ref[0] = (q * (HEAD_DIM ** -0.5 * LOG2_E)).astype(_BF16)
    k_ref[0] = jnp.dot(hn, wk_ref[...], preferred_element_type=_F32).astype(_BF16)
    vt = lax.dot_general(wvt_ref[...], hn, (((1,), (1,)), ((), ())), preferred_element_type=_F32)
    vt_ref[0] = vt.astype(_BF16)


def _mix_in(x, g, w_in, w_vt, layer, chan_hi, chan_lo, *, tm=512):
    b, s, d = x.shape
    const = lambda bi, i: (0, 0)
    row = lambda bi, i: (bi, i, 0)
    full = lambda arr: pl.BlockSpec(arr.shape, const)
    cols = lambda j: pl.BlockSpec((None, d, ATTN_WIDTH), lambda bi, i: (layer, 0, j))
    return pl.pallas_call(
        _mix_in_kernel,
        grid=(b, s // tm),
        in_specs=[pl.BlockSpec((1, tm, d), row), pl.BlockSpec((1, d), const),
                  cols(0), cols(1), cols(2),
                  pl.BlockSpec((None,) + w_vt.shape[1:], lambda bi, i: (layer, 0, 0)),
                  full(chan_hi), full(chan_lo)],
        out_specs=[pl.BlockSpec((1, tm, POOL_WIDTH), row),
                   pl.BlockSpec((1, tm, 2 * FOURIER_WIDTH), row),
                   pl.BlockSpec((1, tm, ATTN_WIDTH), row),
                   pl.BlockSpec((1, tm, ATTN_WIDTH), row),
                   pl.BlockSpec((1, ATTN_WIDTH, tm), lambda bi, i: (bi, 0, i))],
        out_shape=[jax.ShapeDtypeStruct((b, s, POOL_WIDTH), _F32),
                   jax.ShapeDtypeStruct((b, s, 2 * FOURIER_WIDTH), _F32),
                   jax.ShapeDtypeStruct((b, s, ATTN_WIDTH), _BF16),
                   jax.ShapeDtypeStruct((b, s, ATTN_WIDTH), _BF16),
                   jax.ShapeDtypeStruct((b, ATTN_WIDTH, s), _BF16)],
        compiler_params=_params("parallel", "parallel"),
        name="mix_in",
    )(x, g.reshape(1, d), w_in, w_in, w_in, w_vt, chan_hi, chan_lo)


def _dft_tables(s):
    h, l = s // DFT_L, DFT_L
    two_pi = 2.0 * np.pi
    c = np.arange(GROUP_DIM)
    ang = two_pi * np.outer(c, c) / GROUP_DIM
    eye = np.eye(FOURIER_WIDTH // GROUP_DIM)
    chan = np.concatenate([np.kron(eye, np.cos(ang)), np.kron(eye, -np.sin(ang))], axis=1)
    u = np.arange(h)
    ang1 = two_pi * np.outer(u, u) / h
    stage1 = np.concatenate([np.cos(ang1), -np.sin(ang1)], axis=0)
    t = np.arange(l)
    angt = two_pi * np.outer(u, t) / s
    tw_cos, tw_sin = np.cos(angt), np.sin(angt)
    ang2 = two_pi * np.outer(t, t) / l
    scale = 1.0 / math.sqrt(s * GROUP_DIM)
    stage2 = np.concatenate([np.cos(ang2), np.sin(ang2)], axis=1) * scale
    as_f32 = lambda a: jnp.asarray(a, _F32)
    return (_split_hi_lo(as_f32(chan)), _split_hi_lo(as_f32(stage1)), _split_hi_lo(as_f32(stage2)),
            jnp.repeat(as_f32(tw_cos), LANES, axis=1), jnp.repeat(as_f32(tw_sin), LANES, axis=1))


def _dft_stage1_kernel(z_ref, mhi_ref, mlo_ref, twc_ref, tws_ref, o_ref, *, slabs):
    h = z_ref.shape[1]
    w = FOURIER_WIDTH
    z_hi, z_lo = _split_hi_lo(z_ref[0])
    p = _dot3(mhi_ref[...], mlo_ref[...], z_hi, z_lo)
    for j in range(slabs):
        re, im = slice(2 * j * w, (2 * j + 1) * w), slice((2 * j + 1) * w, (2 * j + 2) * w)
        ar = p[:h, re] - p[h:, im]
        ai = p[:h, im] + p[h:, re]
        tw = slice(j * LANES, (j + 1) * LANES)
        tc = jnp.concatenate([twc_ref[:, tw]] * (w // LANES), axis=1)
        ts = jnp.concatenate([tws_ref[:, tw]] * (w // LANES), axis=1)
        o_ref[0, :, re] = ar * tc + ai * ts
        o_ref[0, :, im] = ai * tc - ar * ts


def _dft_stage1(z, m_hi, m_lo, tw_cos, tw_sin, *, slabs=16):
    b, s, zw = z.shape
    h = s // DFT_L
    zv = z.reshape(b, h, DFT_L * zw)
    const = lambda bi, j: (0, 0)
    out = pl.pallas_call(
        functools.partial(_dft_stage1_kernel, slabs=slabs),
        grid=(b, DFT_L // slabs),
        in_specs=[pl.BlockSpec((1, h, slabs * zw), lambda bi, j: (bi, 0, j)),
                  pl.BlockSpec(m_hi.shape, const), pl.BlockSpec(m_lo.shape, const),
                  pl.BlockSpec((h, slabs * LANES), lambda bi, j: (0, j)),
                  pl.BlockSpec((h, slabs * LANES), lambda bi, j: (0, j))],
        out_specs=pl.BlockSpec((1, h, slabs * zw), lambda bi, j: (bi, 0, j)),
        out_shape=jax.ShapeDtypeStruct(zv.shape, _F32),
        compiler_params=_params("parallel", "parallel"),
        name="dft_stage1",
    )(zv, m_hi, m_lo, tw_cos, tw_sin)
    return out.reshape(b, h, DFT_L, zw)


def _dft_stage2_kernel(a_ref, mhi_ref, mlo_ref, fw_ref, o_ref, *, rows):
    w = FOURIER_WIDTH
    for r in range(rows):
        blk = a_ref[0, r]
        stacked = jnp.concatenate([blk[:, :w], blk[:, w:]], axis=0)
        s_hi, s_lo = _split_hi_lo(stacked)
        y = _dot3(mhi_ref[...], mlo_ref[...], s_hi, s_lo)
        yw = jnp.dot(y.astype(_BF16), fw_ref[...], preferred_element_type=_F32)
        o_ref[0, :, r * w:(r + 1) * w] = yw.astype(o_ref.dtype)


def _dft_stage2(a4, m_hi, m_lo, fourier_w, layer, *, rows=8):
    b, h, l, zw = a4.shape
    w = FOURIER_WIDTH
    const = lambda bi, j: (0, 0)
    out = pl.pallas_call(
        functools.partial(_dft_stage2_kernel, rows=rows),
        grid=(b, h // rows),
        in_specs=[pl.BlockSpec((1, rows, l, zw), lambda bi, j: (bi, j, 0, 0)),
                  pl.BlockSpec(m_hi.shape, const), pl.BlockSpec(m_lo.shape, const),
                  pl.BlockSpec((None,) + fourier_w.shape[1:], lambda bi, j: (layer, 0, 0))],
        out_specs=pl.BlockSpec((1, l, rows * w), lambda bi, j: (bi, 0, j)),
        out_shape=jax.ShapeDtypeStruct((b, l, h * w), _BF16),
        compiler_params=_params("parallel", "parallel"),
        name="dft_stage2",
    )(a4, m_hi, m_lo, fourier_w)
    return out.reshape(b, l * h, w)


def _bias_feature_tables(tile):
    pos = np.arange(tile)
    hi, lo = (pos // LANES) * LANES, pos % LANES
    qf = np.zeros((tile, LANES), np.float32)
    kf = np.zeros((tile, LANES), np.float32)
    rest = LOG2_E
    for p in range(3):
        part = float(np.float32(rest).astype(jnp.bfloat16).astype(np.float32))
        rest -= part
        qf[:, 2 * p], qf[:, 2 * p + 1] = hi, lo
        kf[:, 2 * p], kf[:, 2 * p + 1] = part, part
        qf[:, 6 + 2 * p], qf[:, 7 + 2 * p] = -part, -part
        kf[:, 6 + 2 * p], kf[:, 7 + 2 * p] = hi, lo
    return jnp.asarray(qf, _BF16), jnp.asarray(np.stack([-kf, 0.0 * kf, kf]), _BF16)


def _attn_kernel(lam_init_ref, lam_ref, gain_ref, qf_ref, kf_ref, q_ref, k_ref, vt_ref, o_ref,
                 w_ref, acc_ref, p_a, p_b, knorm_ref, *, slope, reach):
    t = q_ref.shape[1]
    n = k_ref.shape[1] // t
    qi = pl.program_id(1)
    c = slope * LOG2_E
    nt = (((1,), (1,)), ((), ()))
    half_rows = (lax.broadcasted_iota(jnp.int32, (8, LANES), 1) // HEAD_DIM
                 == lax.broadcasted_iota(jnp.int32, (8, LANES), 0)).astype(_F32)

    def half_norms(x):
        sq = jnp.square(x.astype(_F32))
        return jnp.sqrt(lax.dot_general(half_rows, sq, nt, preferred_element_type=_F32)[0:2])

    @pl.when(qi == 0)
    def _():
        knorm_ref[0] = jnp.max(half_norms(k_ref[0]))

    qh = q_ref[0]
    lane = lax.broadcasted_iota(jnp.int32, qh.shape, 1)
    zero = jnp.zeros_like(qh)
    halves = jnp.concatenate([jnp.where(lane < HEAD_DIM, qh, zero), jnp.where(lane >= HEAD_DIM, qh, zero)],
                             axis=0)
    qf = (qf_ref[...].astype(_F32) * slope).astype(_BF16)
    w_ref[...] = jnp.concatenate([halves, jnp.concatenate([qf, qf], axis=0)], axis=1)
    qn = half_norms(qh)
    frame = jnp.concatenate([qn[0:1], qn[1:2]], axis=1) * (knorm_ref[0] * FRAME_SLACK) + 0.5

    def qk(tile_idx, side):
        k0 = pl.multiple_of(tile_idx * t, t)
        lhs = jnp.concatenate([k_ref[0, pl.ds(k0, t), :], kf_ref[side]], axis=1)
        return lax.dot_general(lhs, w_ref[...], nt, preferred_element_type=_F32)

    def explicit_bias(tile_delta):
        dist = jnp.abs(lax.broadcasted_iota(jnp.int32, (t, t), 0) - lax.broadcasted_iota(jnp.int32, (t, t), 1)
                       + tile_delta * t)
        bias = dist.astype(_F32) * (-c)
        return jnp.concatenate([bias, bias], axis=1)

    def pv(tile_idx, p_ref):
        k0 = pl.multiple_of(tile_idx * t, t)
        return jnp.dot(vt_ref[0, :, pl.ds(k0, t)], p_ref[...], preferred_element_type=_F32)

    def finish(o_all):
        lam_init = lam_init_ref[0]
        lp = lam_ref[...]
        lam = (jnp.exp(jnp.sum(lp[0:1] * lp[1:2], axis=-1, keepdims=True))
               - jnp.exp(jnp.sum(lp[2:3] * lp[3:4], axis=-1, keepdims=True)) + lam_init)
        o = (o_all[:, :t] - lam * o_all[:, t:]).T
        o = _rms_scale(o, gain_ref[...]) * (1.0 - lam_init)
        o_ref[0] = o.astype(o_ref.dtype)

    if 2 * reach + 1 >= n:
        others = [j + (qi <= j).astype(jnp.int32) for j in range(n - 1)]
        visits = [(qi, 1, None)] + [(i, jnp.where(i < qi, 0, 2), jnp.abs(i - qi).astype(_F32) * (-c * t))
                                    for i in others]
    else:
        visits = [(qi, 1, None)]
        for d in [sign * dist for dist in range(1, reach + 1) for sign in (-1, 1)]:
            i = qi + d
            inside = jnp.logical_and(i >= 0, i < n)
            visits.append((jnp.clip(i, 0, n - 1), 0 if d < 0 else 2,
                           jnp.where(inside, -c * t * abs(d), -NO_TILE)))

    acc_ref[...] = jnp.zeros(acc_ref.shape, _F32)
    l = jnp.zeros((1, 2 * t), _F32)
    bufs = (p_a, p_b)
    for v, (i, side, off) in enumerate(visits):
        if off is None:
            p = jnp.exp2(qk(i, side) + explicit_bias(0) - frame)
        else:
            p = jnp.exp2(qk(i, side) - (frame - off))
        l = l + jnp.sum(p, axis=0, keepdims=True)
        bufs[v % 2][...] = p.astype(_BF16)
        if v > 0:
            acc_ref[...] += pv(visits[v - 1][0], bufs[(v - 1) % 2])
    last = len(visits) - 1
    o_all = (acc_ref[...] + pv(visits[last][0], bufs[last % 2])) / l
    finish(o_all)

    unusable = jnp.logical_or(jnp.logical_not(jnp.sum(o_all * 0.0) == 0.0), jnp.min(l) < MIN_DENOMINATOR)

    @pl.when(unusable)
    def _():
        acc_ref[...] = jnp.zeros(acc_ref.shape, _F32)

        def safe_body(i, carry):
            m, l_run = carry
            sc = qk(i, 1) + explicit_bias(i - qi)
            m_new = jnp.maximum(m, jnp.max(sc, axis=0, keepdims=True))
            alpha = jnp.exp2(m - m_new)
            p = jnp.exp2(sc - m_new)
            p_a[...] = p.astype(_BF16)
            acc_ref[...] = alpha * acc_ref[...] + pv(i, p_a)
            return m_new, alpha * l_run + jnp.sum(p, axis=0, keepdims=True)

        init = (jnp.full((1, 2 * t), -jnp.inf, _F32), jnp.zeros((1, 2 * t), _F32))
        _, l_safe = lax.fori_loop(0, n, safe_body, init)
        finish(acc_ref[...] / l_safe)


def _diff_attention_head(q, k, vt, head, slope, lam_init, lam_params, head_gain, *, tile=512):
    b, s, _ = q.shape
    n = s // tile
    assert n >= 2
    qf, kf = _bias_feature_tables(tile)
    reach = int(math.floor((FLUSH_EXP / (slope * LOG2_E) - 1) / tile)) + 1
    smem = pl.BlockSpec(memory_space=pltpu.SMEM)
    const = lambda shape: pl.BlockSpec(shape, lambda bi, qi: (0,) * len(shape))
    return pl.pallas_call(
        functools.partial(_attn_kernel, slope=slope, reach=reach),
        grid=(b, n),
        in_specs=[smem, const(lam_params.shape),
                  pl.BlockSpec((1, V_DIM), lambda bi, qi: (0, head)),
                  const(qf.shape), const(kf.shape),
                  pl.BlockSpec((1, tile, V_DIM), lambda bi, qi: (bi, qi, head)),
                  pl.BlockSpec((1, s, V_DIM), lambda bi, qi: (bi, 0, head)),
                  pl.BlockSpec((1, V_DIM, s), lambda bi, qi: (bi, head, 0))],
        out_specs=pl.BlockSpec((1, tile, V_DIM), lambda bi, qi: (bi, qi, 0)),
        out_shape=jax.ShapeDtypeStruct((b, s, V_DIM), _BF16),
        scratch_shapes=[pltpu.VMEM((2 * tile, 2 * LANES), _BF16),
                        pltpu.VMEM((V_DIM, 2 * tile), _F32),
                        pltpu.VMEM((tile, 2 * tile), _BF16), pltpu.VMEM((tile, 2 * tile), _BF16),
                        pltpu.SMEM((1,), _F32)],
        compiler_params=_params("arbitrary", "arbitrary"),
        name=f"diff_attn_h{head}",
    )(lam_init, lam_params, head_gain.reshape(1, ATTN_WIDTH), qf, kf, q, k, vt)


def _mix_out_kernel(x_ref, a_ref, a_prev_ref, a_next_ref, pw_ref, ps_ref, yf_ref, *rest):
    head_refs, (wo_ref, o_ref, buf_ref) = rest[:ATTN_HEADS], rest[ATTN_HEADS:]
    i = pl.program_id(1)
    tm = a_ref.shape[1]
    s_len = tm * pl.num_programs(1)
    halo = POOL_HALO
    buf_ref[0:halo] = jnp.where(i > 0, a_prev_ref[0], 0.0)
    buf_ref[halo:halo + tm] = a_ref[0]
    buf_ref[halo + tm:] = jnp.where(i < pl.num_programs(1) - 1, a_next_ref[0], 0.0)
    shifted = lambda d: buf_ref[halo + d:halo + d + tm]
    a = shifted(0)

    lane = lax.broadcasted_iota(jnp.int32, (1, POOL_WIDTH), 1)
    pick = lambda vals: functools.reduce(
        lambda acc, gv: jnp.where(lane >= gv[0] * GROUP_DIM, gv[1], acc), list(enumerate(vals))[1:], vals[0])
    sums, total, lo, hi = [], a, 0, 0
    for w in POOL_WINDOWS:
        left, right = w // 2, w - 1 - w // 2
        for d in list(range(-left, lo)) + list(range(hi + 1, right + 1)):
            total = total + shifted(d)
        lo, hi = -left, right
        sums.append(total)
    win = pick(sums)
    t = i * tm + lax.broadcasted_iota(jnp.int32, (tm, POOL_WIDTH), 0)
    left = pick([w // 2 for w in POOL_WINDOWS])
    right = pick([w - 1 - w // 2 for w in POOL_WINDOWS])
    cnt = jnp.minimum(t + right + 1, s_len) - jnp.maximum(t - left, 0)
    mixed = win / cnt.astype(_F32) - a
    y_pool = jnp.dot(mixed.astype(_BF16), pw_ref[...], preferred_element_type=_F32) * ps_ref[...]

    cat = jnp.concatenate([y_pool.astype(_BF16), yf_ref[0]] + [r[0] for r in head_refs], axis=1)
    o_ref[0] = x_ref[0] + jnp.dot(cat, wo_ref[...], preferred_element_type=_F32)


def _mix_out(x, a, pool_bd, pool_scale, y_four, y_heads, w_out, layer, *, tm=512):
    b, s, d = x.shape
    hb = tm // POOL_HALO
    n_halo = s // POOL_HALO
    row = lambda bi, i: (bi, i, 0)
    const = lambda bi, i: (0, 0)
    return pl.pallas_call(
        _mix_out_kernel,
        grid=(b, s // tm),
        in_specs=[pl.BlockSpec((1, tm, d), row),
                  pl.BlockSpec((1, tm, POOL_WIDTH), row),
                  pl.BlockSpec((1, POOL_HALO, POOL_WIDTH), lambda bi, i: (bi, jnp.maximum(i * hb - 1, 0), 0)),
                  pl.BlockSpec((1, POOL_HALO, POOL_WIDTH),
                               lambda bi, i: (bi, jnp.minimum((i + 1) * hb, n_halo - 1), 0)),
                  pl.BlockSpec(pool_bd.shape, const), pl.BlockSpec((1, POOL_WIDTH), const),
                  pl.BlockSpec((1, tm, FOURIER_WIDTH), row)]
                 + [pl.BlockSpec((1, tm, V_DIM), row)] * ATTN_HEADS
                 + [pl.BlockSpec((None,) + w_out.shape[1:], lambda bi, i: (layer, 0, 0))],
        out_specs=pl.BlockSpec((1, tm, d), row),
        out_shape=jax.ShapeDtypeStruct((b, s, d), _F32),
        scratch_shapes=[pltpu.VMEM((tm + 2 * POOL_HALO, POOL_WIDTH), _F32)],
        compiler_params=_params("parallel", "parallel"),
        name="mix_out",
    )(x, a, a, a, pool_bd, pool_scale.reshape(1, POOL_WIDTH), y_four, *y_heads, w_out)


def kernel(x, ffn1_norm, ffn1_w_gate, ffn1_w_up, ffn1_w_down, mix_norm, w_in, pool_w, pool_scale, fourier_w, lam_q1, lam_k1, lam_q2, lam_k2, attn_head_norm, w_out, ffn2_norm, ffn2_w_gate, ffn2_w_up, ffn2_w_down, final_norm):
    b, s, d = x.shape
    depth = w_in.shape[0]
    bf = lambda w: w.astype(_BF16)
    (chan_hi, chan_lo), (m1_hi, m1_lo), (m2_hi, m2_lo), tw_cos, tw_sin = _dft_tables(s)
    alibi = [2.0 ** (-8.0 * (i + 1) / ATTN_HEADS) for i in range(ATTN_HEADS)]
    assert all(math.frexp(v)[0] == 0.5 for v in alibi), "position features rely on power-of-two slopes"
    assert POOL_WIDTH + FOURIER_WIDTH == ATTN_WIDTH
    ffn1 = (bf(ffn1_w_gate), bf(ffn1_w_up), bf(ffn1_w_down))
    ffn2 = (bf(ffn2_w_gate), bf(ffn2_w_up), bf(ffn2_w_down))
    w_in_b, w_out_b, fourier_b = bf(w_in), bf(w_out), bf(fourier_w)
    w_vt_b = jnp.swapaxes(w_in_b[:, :, 3 * ATTN_WIDTH:], 1, 2)

    for l in range(depth):
        x = _ffn(x.reshape(b * s, d), ffn1_norm[l], *ffn1, l).reshape(b, s, d)

        a, z, q, k, vt = _mix_in(x, mix_norm[l], w_in_b, w_vt_b, l, chan_hi, chan_lo)
        y_four = _dft_stage2(_dft_stage1(z, m1_hi, m1_lo, tw_cos, tw_sin), m2_hi, m2_lo, fourier_b, l)
        lam_init = jnp.full((1,), 0.8 - 0.6 * math.exp(-0.3 * l), _F32)
        lam_params = jnp.stack([lam_q1[l], lam_k1[l], lam_q2[l], lam_k2[l]]).astype(_F32)
        y_heads = [_diff_attention_head(q, k, vt, h, alibi[h], lam_init, lam_params, attn_head_norm[l])
                   for h in range(ATTN_HEADS)]
        pool_bd = jax.scipy.linalg.block_diag(*[pool_w[l, g] for g in range(len(POOL_WINDOWS))])
        x = _mix_out(x, a, bf(pool_bd), pool_scale[l], y_four, y_heads, w_out_b, l)

        x = _ffn(x.reshape(b * s, d), ffn2_norm[l], *ffn2, l,
                 final_norm if l == depth - 1 else None).reshape(b, s, d)
    return x
```

```python
import functools
import math

import numpy as np
import jax
import jax.numpy as jnp
from jax import lax
from jax.experimental import pallas as pl
from jax.experimental.pallas import tpu as pltpu

_F32 = jnp.float32
_BF16 = jnp.bfloat16

NORM_EPS = 1e-6
POOL_WIDTH = 256
FOURIER_WIDTH = 256
ATTN_WIDTH = 512
POOL_WINDOWS = (2, 4, 8, 16)
GROUP_DIM = 64
ATTN_HEADS = 4
HEAD_DIM = 64
V_DIM = 2 * HEAD_DIM
POOL_HALO = 8
DFT_L = 128
LANES = 128
LOG2_E = math.log2(math.e)
FLUSH_EXP = 127.0
FRAME_SLACK = 1.0 + 2.0 ** -5
NO_TILE = 1e30
MIN_DENOMINATOR = 2.0 ** -90

_VMEM_LIMIT = 56 * 1024 * 1024


def _params(*semantics, flags=None):
    return pltpu.CompilerParams(dimension_semantics=semantics, vmem_limit_bytes=_VMEM_LIMIT, flags=flags)


def _rms_scale(x, g):
    ms = jnp.mean(x * x, axis=-1, keepdims=True)
    return x * lax.rsqrt(ms + NORM_EPS) * g


def _split_hi_lo(x):
    hi = x.astype(_BF16)
    lo = (x - hi.astype(_F32)).astype(_BF16)
    return hi, lo


def _dot3(a_hi, a_lo, b_hi, b_lo):
    d = functools.partial(jnp.dot, preferred_element_type=_F32)
    return d(a_hi, b_hi) + (d(a_lo, b_hi) + d(a_hi, b_lo))


def _swiglu_half_step(x, g_ref, wg_ref, wu_ref, wd_ref, ff_chunk):
    hn = _rms_scale(x, g_ref[...]).astype(_BF16)
    acc = jnp.zeros(x.shape, _F32)
    for c in range(wg_ref.shape[1] // ff_chunk):
        sl = slice(c * ff_chunk, (c + 1) * ff_chunk)
        gate = jnp.dot(hn, wg_ref[:, sl], preferred_element_type=_F32)
        up = jnp.dot(hn, wu_ref[:, sl], preferred_element_type=_F32)
        act = (gate / (1.0 + jnp.exp(-gate)) * up).astype(_BF16)
        acc = acc + jnp.dot(act, wd_ref[sl, :], preferred_element_type=_F32)
    return x + 0.5 * acc


def _ffn_kernel(x_ref, g_ref, wg_ref, wu_ref, wd_ref, o_ref, *, ff_chunk):
    o_ref[...] = _swiglu_half_step(x_ref[...], g_ref, wg_ref, wu_ref, wd_ref, ff_chunk)


def _ffn(x2d, g, wg, wu, wd, layer, *, tm=1024, ff_chunk=256):
    n, d = x2d.shape
    d_ff = wg.shape[2]
    const = lambda i: (0, 0)
    resident = lambda shape: pl.BlockSpec((None,) + shape, lambda i: (layer, 0, 0), pipeline_mode=pl.Buffered(1))
    return pl.pallas_call(
        functools.partial(_ffn_kernel, ff_chunk=ff_chunk),
        grid=(n // tm,),
        in_specs=[pl.BlockSpec((tm, d), lambda i: (i, 0)), pl.BlockSpec((1, d), const),
                  resident((d, d_ff)), resident((d, d_ff)), resident((d_ff, d))],
        out_specs=pl.BlockSpec((tm, d), lambda i: (i, 0)),
        out_shape=jax.ShapeDtypeStruct((n, d), _F32),
        compiler_params=_params("parallel"),
        name="ffn",
    )(x2d, g.reshape(1, d), wg, wu, wd)


def _mix_in_kernel(x_ref, g_ref, waf_ref, wq_ref, wk_ref, wvt_ref, chi_ref, clo_ref,
                   a_ref, z_ref, q_ref, k_ref, vt_ref):
    hn = _rms_scale(x_ref[0], g_ref[...]).astype(_BF16)
    af = jnp.dot(hn, waf_ref[...], preferred_element_type=_F32)
    a_ref[0] = af[:, :POOL_WIDTH]
    f_hi, f_lo = _split_hi_lo(af[:, POOL_WIDTH:])
    z_ref[0] = _dot3(f_hi, f_lo, chi_ref[...], clo_ref[...])
    q = jnp.dot(hn, wq_ref[...], preferred_element_type=_F32)
    q_ref[0] = (q * (HEAD_DIM ** -0.5 * LOG2_E)).astype(_BF16)
    k_ref[0] = jnp.dot(hn, wk_ref[...], preferred_element_type=_F32).astype(_BF16)
    vt = lax.dot_general(wvt_ref[...], hn, (((1,), (1,)), ((), ())), preferred_element_type=_F32)
    vt_ref[0] = vt.astype(_BF16)


def _mix_in(x, g, w_in, w_vt, layer, chan_hi, chan_lo, *, tm=512):
    b, s, d = x.shape
    const = lambda bi, i: (0, 0)
    row = lambda bi, i: (bi, i, 0)
    full = lambda arr: pl.BlockSpec(arr.shape, const)
    cols = lambda j: pl.BlockSpec((None, d, ATTN_WIDTH), lambda bi, i: (layer, 0, j))
    return pl.pallas_call(
        _mix_in_kernel,
        grid=(b, s // tm),
        in_specs=[pl.BlockSpec((1, tm, d), row), pl.BlockSpec((1, d), const),
                  cols(0), cols(1), cols(2),
                  pl.BlockSpec((None,) + w_vt.shape[1:], lambda bi, i: (layer, 0, 0)),
                  full(chan_hi), full(chan_lo)],
        out_specs=[pl.BlockSpec((1, tm, POOL_WIDTH), row),
                   pl.BlockSpec((1, tm, 2 * FOURIER_WIDTH), row),
                   pl.BlockSpec((1, tm, ATTN_WIDTH), row),
                   pl.BlockSpec((1, tm, ATTN_WIDTH), row),
                   pl.BlockSpec((1, ATTN_WIDTH, tm), lambda bi, i: (bi, 0, i))],
        out_shape=[jax.ShapeDtypeStruct((b, s, POOL_WIDTH), _F32),
                   jax.ShapeDtypeStruct((b, s, 2 * FOURIER_WIDTH), _F32),
                   jax.ShapeDtypeStruct((b, s, ATTN_WIDTH), _BF16),
                   jax.ShapeDtypeStruct((b, s, ATTN_WIDTH), _BF16),
                   jax.ShapeDtypeStruct((b, ATTN_WIDTH, s), _BF16)],
        compiler_params=_params("parallel", "parallel"),
        name="mix_in",
    )(x, g.reshape(1, d), w_in, w_in, w_in, w_vt, chan_hi, chan_lo)


def _dft_tables(s):
    h, l = s // DFT_L, DFT_L
    two_pi = 2.0 * np.pi
    c = np.arange(GROUP_DIM)
    ang = two_pi * np.outer(c, c) / GROUP_DIM
    eye = np.eye(FOURIER_WIDTH // GROUP_DIM)
    chan = np.concatenate([np.kron(eye, np.cos(ang)), np.kron(eye, -np.sin(ang))], axis=1)
    u = np.arange(h)
    ang1 = two_pi * np.outer(u, u) / h
    stage1 = np.concatenate([np.cos(ang1), -np.sin(ang1)], axis=0)
    t = np.arange(l)
    angt = two_pi * np.outer(u, t) / s
    tw_cos, tw_sin = np.cos(angt), np.sin(angt)
    ang2 = two_pi * np.outer(t, t) / l
    scale = 1.0 / math.sqrt(s * GROUP_DIM)
    stage2 = np.concatenate([np.cos(ang2), np.sin(ang2)], axis=1) * scale
    as_f32 = lambda a: jnp.asarray(a, _F32)
    return (_split_hi_lo(as_f32(chan)), _split_hi_lo(as_f32(stage1)), _split_hi_lo(as_f32(stage2)),
            jnp.repeat(as_f32(tw_cos), LANES, axis=1), jnp.repeat(as_f32(tw_sin), LANES, axis=1))


def _dft_stage1_kernel(z_ref, mhi_ref, mlo_ref, twc_ref, tws_ref, o_ref, *, slabs):
    h = z_ref.shape[1]
    w = FOURIER_WIDTH
    z = jnp.concatenate([z_ref[0, :, j, :] for j in range(slabs)], axis=1)
    z_hi, z_lo = _split_hi_lo(z)
    p = _dot3(mhi_ref[...], mlo_ref[...], z_hi, z_lo)
    for j in range(slabs):
        re, im = slice(2 * j * w, (2 * j + 1) * w), slice((2 * j + 1) * w, (2 * j + 2) * w)
        ar = p[:h, re] - p[h:, im]
        ai = p[:h, im] + p[h:, re]
        tw = slice(j * LANES, (j + 1) * LANES)
        tc = jnp.concatenate([twc_ref[:, tw]] * (w // LANES), axis=1)
        ts = jnp.concatenate([tws_ref[:, tw]] * (w // LANES), axis=1)
        o_ref[0, :, j, :] = jnp.concatenate([ar * tc + ai * ts, ai * tc - ar * ts], axis=1)


def _dft_stage1(z, m_hi, m_lo, tw_cos, tw_sin, *, slabs=16):
    b, s, zw = z.shape
    h = s // DFT_L
    z4 = z.reshape(b, h, DFT_L, zw)
    const = lambda bi, j: (0, 0)
    slab_block = pl.BlockSpec((1, h, slabs, zw), lambda bi, j: (bi, 0, j, 0))
    return pl.pallas_call(
        functools.partial(_dft_stage1_kernel, slabs=slabs),
        grid=(b, DFT_L // slabs),
        in_specs=[slab_block,
                  pl.BlockSpec(m_hi.shape, const), pl.BlockSpec(m_lo.shape, const),
                  pl.BlockSpec((h, slabs * LANES), lambda bi, j: (0, j)),
                  pl.BlockSpec((h, slabs * LANES), lambda bi, j: (0, j))],
        out_specs=slab_block,
        out_shape=jax.ShapeDtypeStruct(z4.shape, _F32),
        compiler_params=_params("parallel", "parallel"),
        name="dft_stage1",
    )(z4, m_hi, m_lo, tw_cos, tw_sin)


def _dft_stage2_kernel(a_ref, mhi_ref, mlo_ref, fw_ref, o_ref, *, rows):
    w = FOURIER_WIDTH
    for r in range(rows):
        blk = a_ref[0, r]
        stacked = jnp.concatenate([blk[:, :w], blk[:, w:]], axis=0)
        s_hi, s_lo = _split_hi_lo(stacked)
        y = _dot3(mhi_ref[...], mlo_ref[...], s_hi, s_lo)
        o_ref[0, :, r, :] = jnp.dot(y.astype(_BF16), fw_ref[...], preferred_element_type=_F32)


def _dft_stage2(a4, m_hi, m_lo, fourier_w, layer, *, rows=8):
    b, h, l, zw = a4.shape
    w = FOURIER_WIDTH
    const = lambda bi, j: (0, 0)
    out = pl.pallas_call(
        functools.partial(_dft_stage2_kernel, rows=rows),
        grid=(b, h // rows),
        in_specs=[pl.BlockSpec((1, rows, l, zw), lambda bi, j: (bi, j, 0, 0)),
                  pl.BlockSpec(m_hi.shape, const), pl.BlockSpec(m_lo.shape, const),
                  pl.BlockSpec((None,) + fourier_w.shape[1:], lambda bi, j: (layer, 0, 0))],
        out_specs=pl.BlockSpec((1, l, rows, w), lambda bi, j: (bi, 0, j, 0)),
        out_shape=jax.ShapeDtypeStruct((b, l, h, w), _F32),
        compiler_params=_params("parallel", "parallel"),
        name="dft_stage2",
    )(a4, m_hi, m_lo, fourier_w)
    return out.reshape(b, l * h, w)


def _bias_feature_tables(tile):
    pos = np.arange(tile)
    hi, lo = (pos // LANES) * LANES, pos % LANES
    qf = np.zeros((tile, LANES), np.float32)
    kf = np.zeros((tile, LANES), np.float32)
    rest = LOG2_E
    for p in range(3):
        part = float(np.float32(rest).astype(jnp.bfloat16).astype(np.float32))
        rest -= part
        qf[:, 2 * p], qf[:, 2 * p + 1] = hi, lo
        kf[:, 2 * p], kf[:, 2 * p + 1] = part, part
        qf[:, 6 + 2 * p], qf[:, 7 + 2 * p] = -part, -part
        kf[:, 6 + 2 * p], kf[:, 7 + 2 * p] = hi, lo
    return jnp.asarray(qf, _BF16), jnp.asarray(np.stack([-kf, 0.0 * kf, kf]), _BF16)


def _attn_kernel(lam_init_ref, lam_ref, gain_ref, qf_ref, kf_ref, q_ref, k_ref, vt_ref, o_ref,
                 w_ref, acc_ref, p_a, p_b, knorm_ref, *, slope, reach, split_qk):
    t = q_ref.shape[1]
    n = k_ref.shape[1] // t
    qi = pl.program_id(1)
    c = slope * LOG2_E
    nt = (((1,), (1,)), ((), ()))
    half_rows = (lax.broadcasted_iota(jnp.int32, (8, LANES), 1) // HEAD_DIM
                 == lax.broadcasted_iota(jnp.int32, (8, LANES), 0)).astype(_F32)

    def half_norms(x):
        sq = jnp.square(x.astype(_F32))
        return jnp.sqrt(lax.dot_general(half_rows, sq, nt, preferred_element_type=_F32)[0:2])

    @pl.when(qi == 0)
    def _():
        knorm_ref[0] = jnp.max(half_norms(k_ref[0]))

    qh = q_ref[0]
    lane = lax.broadcasted_iota(jnp.int32, qh.shape, 1)
    zero = jnp.zeros_like(qh)
    halves = jnp.concatenate([jnp.where(lane < HEAD_DIM, qh, zero), jnp.where(lane >= HEAD_DIM, qh, zero)],
                             axis=0)
    qf = (qf_ref[...].astype(_F32) * slope).astype(_BF16)
    w_ref[...] = jnp.concatenate([halves, jnp.concatenate([qf, qf], axis=0)], axis=1)
    qn = half_norms(qh)
    frame = jnp.concatenate([qn[0:1], qn[1:2]], axis=1) * (knorm_ref[0] * FRAME_SLACK) + 0.5

    def qk(tile_idx, side):
        k0 = pl.multiple_of(tile_idx * t, t)
        lhs = jnp.concatenate([k_ref[0, pl.ds(k0, t), :], kf_ref[side]], axis=1)
        if split_qk:
            return jnp.concatenate([lax.dot_general(lhs, w_ref[:t], nt, preferred_element_type=_F32),
                                    lax.dot_general(lhs, w_ref[t:], nt, preferred_element_type=_F32)], axis=1)
        return lax.dot_general(lhs, w_ref[...], nt, preferred_element_type=_F32)

    def explicit_bias(tile_delta):
        dist = jnp.abs(lax.broadcasted_iota(jnp.int32, (t, t), 0) - lax.broadcasted_iota(jnp.int32, (t, t), 1)
                       + tile_delta * t)
        bias = dist.astype(_F32) * (-c)
        return jnp.concatenate([bias, bias], axis=1)

    def pv(tile_idx, p_ref):
        k0 = pl.multiple_of(tile_idx * t, t)
        return jnp.dot(vt_ref[0, :, pl.ds(k0, t)], p_ref[...], preferred_element_type=_F32)

    def finish(o_all):
        lam_init = lam_init_ref[0]
        lp = lam_ref[...]
        lam = (jnp.exp(jnp.sum(lp[0:1] * lp[1:2], axis=-1, keepdims=True))
               - jnp.exp(jnp.sum(lp[2:3] * lp[3:4], axis=-1, keepdims=True)) + lam_init)
        o = (o_all[:, :t] - lam * o_all[:, t:]).T
        o = _rms_scale(o, gain_ref[...]) * (1.0 - lam_init)
        o_ref[0] = o.astype(o_ref.dtype)

    if 2 * reach + 1 >= n:
        others = [j + (qi <= j).astype(jnp.int32) for j in range(n - 1)]
        visits = [(qi, 1, None)] + [(i, jnp.where(i < qi, 0, 2), jnp.abs(i - qi).astype(_F32) * (-c * t))
                                    for i in others]
    else:
        visits = [(qi, 1, None)]
        for d in [sign * dist for dist in range(1, reach + 1) for sign in (-1, 1)]:
            i = qi + d
            inside = jnp.logical_and(i >= 0, i < n)
            visits.append((jnp.clip(i, 0, n - 1), 0 if d < 0 else 2,
                           jnp.where(inside, -c * t * abs(d), -NO_TILE)))

    acc_ref[...] = jnp.zeros(acc_ref.shape, _F32)
    l = jnp.zeros((1, 2 * t), _F32)
    bufs = (p_a, p_b)
    for v, (i, side, off) in enumerate(visits):
        if off is None:
            p = jnp.exp2(qk(i, side) + explicit_bias(0) - frame)
        else:
            p = jnp.exp2(qk(i, side) - (frame - off))
        l = l + jnp.sum(p, axis=0, keepdims=True)
        bufs[v % 2][...] = p.astype(_BF16)
        if v > 0:
            acc_ref[...] += pv(visits[v - 1][0], bufs[(v - 1) % 2])
    last = len(visits) - 1
    o_all = (acc_ref[...] + pv(visits[last][0], bufs[last % 2])) / l
    finish(o_all)

    unusable = jnp.logical_or(jnp.logical_not(jnp.sum(o_all * 0.0) == 0.0), jnp.min(l) < MIN_DENOMINATOR)

    @pl.when(unusable)
    def _():
        acc_ref[...] = jnp.zeros(acc_ref.shape, _F32)

        def safe_body(i, carry):
            m, l_run = carry
            sc = qk(i, 1) + explicit_bias(i - qi)
            m_new = jnp.maximum(m, jnp.max(sc, axis=0, keepdims=True))
            alpha = jnp.exp2(m - m_new)
            p = jnp.exp2(sc - m_new)
            p_a[...] = p.astype(_BF16)
            acc_ref[...] = alpha * acc_ref[...] + pv(i, p_a)
            return m_new, alpha * l_run + jnp.sum(p, axis=0, keepdims=True)

        init = (jnp.full((1, 2 * t), -jnp.inf, _F32), jnp.zeros((1, 2 * t), _F32))
        _, l_safe = lax.fori_loop(0, n, safe_body, init)
        finish(acc_ref[...] / l_safe)


def _diff_attention_head(q, k, vt, head, slope, lam_init, lam_params, head_gain, *, tile=512):
    b, s, _ = q.shape
    n = s // tile
    assert n >= 2
    qf, kf = _bias_feature_tables(tile)
    reach = int(math.floor((FLUSH_EXP / (slope * LOG2_E) - 1) / tile)) + 1
    smem = pl.BlockSpec(memory_space=pltpu.SMEM)
    const = lambda shape: pl.BlockSpec(shape, lambda bi, qi: (0,) * len(shape))
    return pl.pallas_call(
        functools.partial(_attn_kernel, slope=slope, reach=reach, split_qk=head == 3),
        grid=(b, n),
        in_specs=[smem, const(lam_params.shape),
                  pl.BlockSpec((1, V_DIM), lambda bi, qi: (0, head)),
                  const(qf.shape), const(kf.shape),
                  pl.BlockSpec((1, tile, V_DIM), lambda bi, qi: (bi, qi, head)),
                  pl.BlockSpec((1, s, V_DIM), lambda bi, qi: (bi, 0, head)),
                  pl.BlockSpec((1, V_DIM, s), lambda bi, qi: (bi, head, 0))],
        out_specs=pl.BlockSpec((1, tile, V_DIM), lambda bi, qi: (bi, qi, 0)),
        out_shape=jax.ShapeDtypeStruct((b, s, V_DIM), _BF16),
        scratch_shapes=[pltpu.VMEM((2 * tile, 2 * LANES), _BF16),
                        pltpu.VMEM((V_DIM, 2 * tile), _F32),
                        pltpu.VMEM((tile, 2 * tile), _BF16), pltpu.VMEM((tile, 2 * tile), _BF16),
                        pltpu.SMEM((1,), _F32)],
        compiler_params=_params("arbitrary", "arbitrary"),
        name=f"diff_attn_h{head}",
    )(lam_init, lam_params, head_gain.reshape(1, ATTN_WIDTH), qf, kf, q, k, vt)


def _mix_out_ffn_kernel(x_ref, a_ref, a_prev_ref, a_next_ref, pw_ref, ps_ref, yf_ref, *rest, ff_chunk, final):
    head_refs, rest = rest[:ATTN_HEADS], rest[ATTN_HEADS:]
    if final:
        wo_ref, g_ref, wg_ref, wu_ref, wd_ref, gf_ref, o_ref, buf_ref = rest
    else:
        wo_ref, g_ref, wg_ref, wu_ref, wd_ref, o_ref, buf_ref = rest
    i = pl.program_id(1)
    tm = a_ref.shape[1]
    s_len = tm * pl.num_programs(1)
    halo = POOL_HALO
    buf_ref[0:halo] = jnp.where(i > 0, a_prev_ref[0], 0.0)
    buf_ref[halo:halo + tm] = a_ref[0]
    buf_ref[halo + tm:] = jnp.where(i < pl.num_programs(1) - 1, a_next_ref[0], 0.0)
    lane = lax.broadcasted_iota(jnp.int32, (1, LANES), 1)
    pick = lambda v0, v1: jnp.where(lane >= GROUP_DIM, v1, v0)

    def pooled(lanes, windows):
        shifted = lambda d: buf_ref[halo + d:halo + d + tm, lanes]
        a = shifted(0)
        sums, total, lo, hi = [], a, 0, 0
        for w in windows:
            left, right = w // 2, w - 1 - w // 2
            for d in list(range(-left, lo)) + list(range(hi + 1, right + 1)):
                total = total + shifted(d)
            lo, hi = -left, right
            sums.append(total)
        win = pick(*sums)
        mixed = win * pick(*[1.0 / w for w in windows]) - a

        def edge(rows):
            t = i * tm + rows.start + lax.broadcasted_iota(jnp.int32, (halo, LANES), 0)
            left, right = pick(*[w // 2 for w in windows]), pick(*[w - 1 - w // 2 for w in windows])
            cnt = jnp.minimum(t + right + 1, s_len) - jnp.maximum(t - left, 0)
            return win[rows] / cnt.astype(_F32) - a[rows]

        return jnp.concatenate([edge(slice(0, halo)), mixed[halo:tm - halo], edge(slice(tm - halo, tm))], axis=0)

    mixed = jnp.concatenate([pooled(slice(0, LANES), POOL_WINDOWS[:2]),
                             pooled(slice(LANES, 2 * LANES), POOL_WINDOWS[2:])], axis=1)
    y_pool = jnp.dot(mixed.astype(_BF16), pw_ref[...], preferred_element_type=_F32) * ps_ref[...]

    rest = jnp.concatenate([yf_ref[0].astype(_BF16)] + [r[0] for r in head_refs], axis=1)
    y = jnp.dot(rest, wo_ref[POOL_WIDTH:, :], preferred_element_type=_F32)
    y = y + jnp.dot(y_pool.astype(_BF16), wo_ref[:POOL_WIDTH, :], preferred_element_type=_F32)
    x = _swiglu_half_step(x_ref[0] + y, g_ref, wg_ref, wu_ref, wd_ref, ff_chunk)
    o_ref[0] = _rms_scale(x, gf_ref[...]) if final else x


def _mix_out_ffn(x, a, pool_bd, pool_scale, y_four, y_heads, w_out, g, wg, wu, wd, layer, final_g=None,
                 *, tm=512, ff_chunk=256):
    b, s, d = x.shape
    d_ff = wg.shape[2]
    hb = tm // POOL_HALO
    n_halo = s // POOL_HALO
    row = lambda bi, i: (bi, i, 0)
    const = lambda bi, i: (0, 0)
    resident = lambda shape: pl.BlockSpec((None,) + shape, lambda bi, i: (layer, 0, 0),
                                          pipeline_mode=pl.Buffered(1))
    in_specs = ([pl.BlockSpec((1, tm, d), row),
                 pl.BlockSpec((1, tm, POOL_WIDTH), row),
                 pl.BlockSpec((1, POOL_HALO, POOL_WIDTH), lambda bi, i: (bi, jnp.maximum(i * hb - 1, 0), 0)),
                 pl.BlockSpec((1, POOL_HALO, POOL_WIDTH),
                              lambda bi, i: (bi, jnp.minimum((i + 1) * hb, n_halo - 1), 0)),
                 pl.BlockSpec(pool_bd.shape, const), pl.BlockSpec((1, POOL_WIDTH), const),
                 pl.BlockSpec((1, tm, FOURIER_WIDTH), row)]
                + [pl.BlockSpec((1, tm, V_DIM), row)] * ATTN_HEADS
                + [resident((d, d)), pl.BlockSpec((1, d), const),
                   resident((d, d_ff)), resident((d, d_ff)), resident((d_ff, d))])
    args = [x, a, a, a, pool_bd, pool_scale.reshape(1, POOL_WIDTH), y_four, *y_heads, w_out,
            g.reshape(1, d), wg, wu, wd]
    if final_g is not None:
        in_specs.append(pl.BlockSpec((1, d), const))
        args.append(final_g.reshape(1, d))
    return pl.pallas_call(
        functools.partial(_mix_out_ffn_kernel, ff_chunk=ff_chunk, final=final_g is not None),
        grid=(b, s // tm),
        in_specs=in_specs,
        out_specs=pl.BlockSpec((1, tm, d), row),
        out_shape=jax.ShapeDtypeStruct((b, s, d), _F32),
        scratch_shapes=[pltpu.VMEM((tm + 2 * POOL_HALO, POOL_WIDTH), _F32)],
        compiler_params=_params("parallel", "parallel"),
        name="mix_out_ffn",
    )(*args)


def kernel(x, ffn1_norm, ffn1_w_gate, ffn1_w_up, ffn1_w_down, mix_norm, w_in, pool_w, pool_scale, fourier_w, lam_q1, lam_k1, lam_q2, lam_k2, attn_head_norm, w_out, ffn2_norm, ffn2_w_gate, ffn2_w_up, ffn2_w_down, final_norm):
    b, s, d = x.shape
    depth = w_in.shape[0]
    bf = lambda w: w.astype(_BF16)
    (chan_hi, chan_lo), (m1_hi, m1_lo), (m2_hi, m2_lo), tw_cos, tw_sin = _dft_tables(s)
    alibi = [2.0 ** (-8.0 * (i + 1) / ATTN_HEADS) for i in range(ATTN_HEADS)]
    assert all(math.frexp(v)[0] == 0.5 for v in alibi), "position features rely on power-of-two slopes"
    assert POOL_WIDTH + FOURIER_WIDTH == ATTN_WIDTH
    assert all(w & (w - 1) == 0 and w <= 2 * POOL_HALO for w in POOL_WINDOWS) and POOL_WIDTH == 2 * LANES
    ffn1 = (bf(ffn1_w_gate), bf(ffn1_w_up), bf(ffn1_w_down))
    ffn2 = (bf(ffn2_w_gate), bf(ffn2_w_up), bf(ffn2_w_down))
    w_in_b, w_out_b, fourier_b = bf(w_in), bf(w_out), bf(fourier_w)
    w_vt_b = jnp.swapaxes(w_in_b[:, :, 3 * ATTN_WIDTH:], 1, 2)

    for l in range(depth):
        x = _ffn(x.reshape(b * s, d), ffn1_norm[l], *ffn1, l).reshape(b, s, d)

        a, z, q, k, vt = _mix_in(x, mix_norm[l], w_in_b, w_vt_b, l, chan_hi, chan_lo)
        y_four = _dft_stage2(_dft_stage1(z, m1_hi, m1_lo, tw_cos, tw_sin), m2_hi, m2_lo, fourier_b, l)
        lam_init = jnp.full((1,), 0.8 - 0.6 * math.exp(-0.3 * l), _F32)
        lam_params = jnp.stack([lam_q1[l], lam_k1[l], lam_q2[l], lam_k2[l]]).astype(_F32)
        y_heads = [_diff_attention_head(q, k, vt, h, alibi[h], lam_init, lam_params, attn_head_norm[l])
                   for h in range(ATTN_HEADS)]
        pool_bd = jax.scipy.linalg.block_diag(*[pool_w[l, g] for g in range(len(POOL_WINDOWS))])
        x = _mix_out_ffn(x, a, bf(pool_bd), pool_scale[l], y_four, y_heads, w_out_b, ffn2_norm[l], *ffn2, l,
                         final_norm if l == depth - 1 else None)
    return x
```

```python
import functools
import math

import numpy as np
import jax
import jax.numpy as jnp
from jax import lax
from jax.experimental import pallas as pl
from jax.experimental.pallas import tpu as pltpu

_F32 = jnp.float32
_BF16 = jnp.bfloat16

NORM_EPS = 1e-6
POOL_WIDTH = 256
FOURIER_WIDTH = 256
ATTN_WIDTH = 512
POOL_WINDOWS = (2, 4, 8, 16)
GROUP_DIM = 64
ATTN_HEADS = 4
HEAD_DIM = 64
V_DIM = 2 * HEAD_DIM
POOL_HALO = 8
DFT_L = 128
LANES = 128
LOG2_E = math.log2(math.e)
FLUSH_EXP = 127.0
FRAME_SLACK = 1.0 + 2.0 ** -5
NO_TILE = 1e30
MIN_DENOMINATOR = 2.0 ** -90

_VMEM_LIMIT = 56 * 1024 * 1024


def _params(*semantics, flags=None):
    return pltpu.CompilerParams(dimension_semantics=semantics, vmem_limit_bytes=_VMEM_LIMIT, flags=flags)


def _rms_scale(x, g):
    ms = jnp.mean(x * x, axis=-1, keepdims=True)
    return x * lax.rsqrt(ms + NORM_EPS) * g


def _split_hi_lo(x):
    hi = x.astype(_BF16)
    lo = (x - hi.astype(_F32)).astype(_BF16)
    return hi, lo


def _dot3(a_hi, a_lo, b_hi, b_lo):
    d = functools.partial(jnp.dot, preferred_element_type=_F32)
    return d(a_hi, b_hi) + (d(a_lo, b_hi) + d(a_hi, b_lo))


def _swiglu_half_step(x, g_ref, wg_ref, wu_ref, wd_ref, ff_chunk):
    hn = _rms_scale(x, g_ref[...]).astype(_BF16)
    acc = jnp.zeros(x.shape, _F32)
    for c in range(wg_ref.shape[1] // ff_chunk):
        sl = slice(c * ff_chunk, (c + 1) * ff_chunk)
        gate = jnp.dot(hn, wg_ref[:, sl], preferred_element_type=_F32)
        up = jnp.dot(hn, wu_ref[:, sl], preferred_element_type=_F32)
        act = (gate / (1.0 + jnp.exp(-gate)) * up).astype(_BF16)
        acc = acc + jnp.dot(act, wd_ref[sl, :], preferred_element_type=_F32)
    return x + 0.5 * acc


def _ffn_mix_in_kernel(x_ref, g1_ref, wg_ref, wu_ref, wd_ref, g_ref, waf_ref, wq_ref, wk_ref, wvt_ref,
                       chi_ref, clo_ref, x_out_ref, a_ref, z_ref, q_ref, k_ref, vt_ref, *, ff_chunk):
    x = _swiglu_half_step(x_ref[0], g1_ref, wg_ref, wu_ref, wd_ref, ff_chunk)
    x_out_ref[0] = x
    hn = _rms_scale(x, g_ref[...]).astype(_BF16)
    af = jnp.dot(hn, waf_ref[...], preferred_element_type=_F32)
    a_ref[0] = af[:, :POOL_WIDTH]
    f_hi, f_lo = _split_hi_lo(af[:, POOL_WIDTH:])
    z_ref[0] = _dot3(f_hi, f_lo, chi_ref[...], clo_ref[...])
    q = jnp.dot(hn, wq_ref[...], preferred_element_type=_F32)
    q_ref[0] = (q * (HEAD_DIM ** -0.5 * LOG2_E)).astype(_BF16)
    k_ref[0] = jnp.dot(hn, wk_ref[...], preferred_element_type=_F32).astype(_BF16)
    vt = lax.dot_general(wvt_ref[...], hn, (((1,), (1,)), ((), ())), preferred_element_type=_F32)
    vt_ref[0] = vt.astype(_BF16)


def _ffn_mix_in(x, g1, wg, wu, wd, g, w_in, w_vt, layer, chan_hi, chan_lo, *, tm=512, ff_chunk=256):
    b, s, d = x.shape
    d_ff = wg.shape[2]
    const = lambda bi, i: (0, 0)
    row = lambda bi, i: (bi, i, 0)
    full = lambda arr: pl.BlockSpec(arr.shape, const)
    once = pl.Buffered(1)
    resident = lambda shape: pl.BlockSpec((None,) + shape, lambda bi, i: (layer, 0, 0), pipeline_mode=once)
    cols = lambda j: pl.BlockSpec((None, d, ATTN_WIDTH), lambda bi, i: (layer, 0, j), pipeline_mode=once)
    return pl.pallas_call(
        functools.partial(_ffn_mix_in_kernel, ff_chunk=ff_chunk),
        grid=(b, s // tm),
        in_specs=[pl.BlockSpec((1, tm, d), row), pl.BlockSpec((1, d), const),
                  resident((d, d_ff)), resident((d, d_ff)), resident((d_ff, d)),
                  pl.BlockSpec((1, d), const), cols(0), cols(1), cols(2), resident(w_vt.shape[1:]),
                  full(chan_hi), full(chan_lo)],
        out_specs=[pl.BlockSpec((1, tm, d), row),
                   pl.BlockSpec((1, tm, POOL_WIDTH), row),
                   pl.BlockSpec((1, tm, 2 * FOURIER_WIDTH), row),
                   pl.BlockSpec((1, tm, ATTN_WIDTH), row),
                   pl.BlockSpec((1, tm, ATTN_WIDTH), row),
                   pl.BlockSpec((1, ATTN_WIDTH, tm), lambda bi, i: (bi, 0, i))],
        out_shape=[jax.ShapeDtypeStruct((b, s, d), _F32),
                   jax.ShapeDtypeStruct((b, s, POOL_WIDTH), _F32),
                   jax.ShapeDtypeStruct((b, s, 2 * FOURIER_WIDTH), _F32),
                   jax.ShapeDtypeStruct((b, s, ATTN_WIDTH), _BF16),
                   jax.ShapeDtypeStruct((b, s, ATTN_WIDTH), _BF16),
                   jax.ShapeDtypeStruct((b, ATTN_WIDTH, s), _BF16)],
        compiler_params=_params("parallel", "parallel"),
        name="ffn_mix_in",
    )(x, g1.reshape(1, d), wg, wu, wd, g.reshape(1, d), w_in, w_in, w_in, w_vt, chan_hi, chan_lo)


def _dft_tables(s):
    h, l = s // DFT_L, DFT_L
    two_pi = 2.0 * np.pi
    c = np.arange(GROUP_DIM)
    ang = two_pi * np.outer(c, c) / GROUP_DIM
    eye = np.eye(FOURIER_WIDTH // GROUP_DIM)
    chan = np.concatenate([np.kron(eye, np.cos(ang)), np.kron(eye, -np.sin(ang))], axis=1)
    u = np.arange(h)
    ang1 = two_pi * np.outer(u, u) / h
    stage1 = np.concatenate([np.cos(ang1), -np.sin(ang1)], axis=0)
    t = np.arange(l)
    angt = two_pi * np.outer(u, t) / s
    tw_cos, tw_sin = np.cos(angt), np.sin(angt)
    ang2 = two_pi * np.outer(t, t) / l
    scale = 1.0 / math.sqrt(s * GROUP_DIM)
    stage2 = np.concatenate([np.cos(ang2), np.sin(ang2)], axis=1) * scale
    as_f32 = lambda a: jnp.asarray(a, _F32)
    return (_split_hi_lo(as_f32(chan)), _split_hi_lo(as_f32(stage1)), _split_hi_lo(as_f32(stage2)),
            jnp.repeat(as_f32(tw_cos), LANES, axis=1), jnp.repeat(as_f32(tw_sin), LANES, axis=1))


def _dft_stage1_kernel(z_ref, mhi_ref, mlo_ref, twc_ref, tws_ref, o_ref, *, slabs):
    h = z_ref.shape[1]
    w = FOURIER_WIDTH
    z = jnp.concatenate([z_ref[0, :, j, :] for j in range(slabs)], axis=1)
    z_hi, z_lo = _split_hi_lo(z)
    p = _dot3(mhi_ref[...], mlo_ref[...], z_hi, z_lo)
    for j in range(slabs):
        re, im = slice(2 * j * w, (2 * j + 1) * w), slice((2 * j + 1) * w, (2 * j + 2) * w)
        ar = p[:h, re] - p[h:, im]
        ai = p[:h, im] + p[h:, re]
        tw = slice(j * LANES, (j + 1) * LANES)
        tc = jnp.concatenate([twc_ref[:, tw]] * (w // LANES), axis=1)
        ts = jnp.concatenate([tws_ref[:, tw]] * (w // LANES), axis=1)
        o_ref[0, :, j, :] = jnp.concatenate([ar * tc + ai * ts, ai * tc - ar * ts], axis=1)


def _dft_stage1(z, m_hi, m_lo, tw_cos, tw_sin, *, slabs=16):
    b, s, zw = z.shape
    h = s // DFT_L
    z4 = z.reshape(b, h, DFT_L, zw)
    const = lambda bi, j: (0, 0)
    slab_block = pl.BlockSpec((1, h, slabs, zw), lambda bi, j: (bi, 0, j, 0))
    return pl.pallas_call(
        functools.partial(_dft_stage1_kernel, slabs=slabs),
        grid=(b, DFT_L // slabs),
        in_specs=[slab_block,
                  pl.BlockSpec(m_hi.shape, const), pl.BlockSpec(m_lo.shape, const),
                  pl.BlockSpec((h, slabs * LANES), lambda bi, j: (0, j)),
                  pl.BlockSpec((h, slabs * LANES), lambda bi, j: (0, j))],
        out_specs=slab_block,
        out_shape=jax.ShapeDtypeStruct(z4.shape, _F32),
        compiler_params=_params("parallel", "parallel"),
        name="dft_stage1",
    )(z4, m_hi, m_lo, tw_cos, tw_sin)


def _dft_stage2_kernel(a_ref, mhi_ref, mlo_ref, fw_ref, o_ref, *, rows):
    w = FOURIER_WIDTH
    for r in range(rows):
        blk = a_ref[0, r]
        stacked = jnp.concatenate([blk[:, :w], blk[:, w:]], axis=0)
        s_hi, s_lo = _split_hi_lo(stacked)
        y = _dot3(mhi_ref[...], mlo_ref[...], s_hi, s_lo)
        o_ref[0, :, r, :] = jnp.dot(y.astype(_BF16), fw_ref[...], preferred_element_type=_F32)


def _dft_stage2(a4, m_hi, m_lo, fourier_w, layer, *, rows=8):
    b, h, l, zw = a4.shape
    w = FOURIER_WIDTH
    const = lambda bi, j: (0, 0)
    out = pl.pallas_call(
        functools.partial(_dft_stage2_kernel, rows=rows),
        grid=(b, h // rows),
        in_specs=[pl.BlockSpec((1, rows, l, zw), lambda bi, j: (bi, j, 0, 0)),
                  pl.BlockSpec(m_hi.shape, const), pl.BlockSpec(m_lo.shape, const),
                  pl.BlockSpec((None,) + fourier_w.shape[1:], lambda bi, j: (layer, 0, 0))],
        out_specs=pl.BlockSpec((1, l, rows, w), lambda bi, j: (bi, 0, j, 0)),
        out_shape=jax.ShapeDtypeStruct((b, l, h, w), _F32),
        compiler_params=_params("parallel", "parallel"),
        name="dft_stage2",
    )(a4, m_hi, m_lo, fourier_w)
    return out.reshape(b, l * h, w)


def _bias_feature_tables(tile):
    pos = np.arange(tile)
    hi, lo = (pos // LANES) * LANES, pos % LANES
    qf = np.zeros((tile, LANES), np.float32)
    kf = np.zeros((tile, LANES), np.float32)
    rest = LOG2_E
    for p in range(3):
        part = float(np.float32(rest).astype(jnp.bfloat16).astype(np.float32))
        rest -= part
        qf[:, 2 * p], qf[:, 2 * p + 1] = hi, lo
        kf[:, 2 * p], kf[:, 2 * p + 1] = part, part
        qf[:, 6 + 2 * p], qf[:, 7 + 2 * p] = -part, -part
        kf[:, 6 + 2 * p], kf[:, 7 + 2 * p] = hi, lo
    return jnp.asarray(qf, _BF16), jnp.asarray(np.stack([-kf, 0.0 * kf, kf]), _BF16)


def _attn_kernel(lam_init_ref, lam_ref, gain_ref, qf_ref, kf_ref, q_ref, k_ref, vt_ref, o_ref,
                 w_ref, acc_ref, p_a, p_b, knorm_ref, *, slope, reach):
    t = q_ref.shape[1]
    n = k_ref.shape[1] // t
    qi = pl.program_id(1)
    c = slope * LOG2_E
    nt = (((1,), (1,)), ((), ()))
    half_rows = (lax.broadcasted_iota(jnp.int32, (8, LANES), 1) // HEAD_DIM
                 == lax.broadcasted_iota(jnp.int32, (8, LANES), 0)).astype(_F32)

    def half_norms(x):
        sq = jnp.square(x.astype(_F32))
        return jnp.sqrt(lax.dot_general(half_rows, sq, nt, preferred_element_type=_F32)[0:2])

    @pl.when(qi == 0)
    def _():
        knorm_ref[0] = jnp.max(half_norms(k_ref[0]))

    qh = q_ref[0]
    lane = lax.broadcasted_iota(jnp.int32, qh.shape, 1)
    zero = jnp.zeros_like(qh)
    halves = jnp.concatenate([jnp.where(lane < HEAD_DIM, qh, zero), jnp.where(lane >= HEAD_DIM, qh, zero)],
                             axis=0)
    qf = (qf_ref[...].astype(_F32) * slope).astype(_BF16)
    w_ref[...] = jnp.concatenate([halves, jnp.concatenate([qf, qf], axis=0)], axis=1)
    qn = half_norms(qh)
    frame = jnp.concatenate([qn[0:1], qn[1:2]], axis=1) * (knorm_ref[0] * FRAME_SLACK) + 0.5

    def qk(tile_idx, side):
        k0 = pl.multiple_of(tile_idx * t, t)
        lhs = jnp.concatenate([k_ref[0, pl.ds(k0, t), :], kf_ref[side]], axis=1)
        return lax.dot_general(lhs, w_ref[...], nt, preferred_element_type=_F32)

    def explicit_bias(tile_delta):
        dist = jnp.abs(lax.broadcasted_iota(jnp.int32, (t, t), 0) - lax.broadcasted_iota(jnp.int32, (t, t), 1)
                       + tile_delta * t)
        bias = dist.astype(_F32) * (-c)
        return jnp.concatenate([bias, bias], axis=1)

    def pv(tile_idx, p_ref):
        k0 = pl.multiple_of(tile_idx * t, t)
        return jnp.dot(vt_ref[0, :, pl.ds(k0, t)], p_ref[...], preferred_element_type=_F32)

    def finish(o_all):
        lam_init = lam_init_ref[0]
        lp = lam_ref[...]
        lam = (jnp.exp(jnp.sum(lp[0:1] * lp[1:2], axis=-1, keepdims=True))
               - jnp.exp(jnp.sum(lp[2:3] * lp[3:4], axis=-1, keepdims=True)) + lam_init)
        o = (o_all[:, :t] - lam * o_all[:, t:]).T
        o = _rms_scale(o, gain_ref[...]) * (1.0 - lam_init)
        o_ref[0] = o.astype(o_ref.dtype)

    if 2 * reach + 1 >= n:
        others = [j + (qi <= j).astype(jnp.int32) for j in range(n - 1)]
        visits = [(qi, 1, None)] + [(i, jnp.where(i < qi, 0, 2), jnp.abs(i - qi).astype(_F32) * (-c * t))
                                    for i in others]
    else:
        visits = [(qi, 1, None)]
        for d in [sign * dist for dist in range(1, reach + 1) for sign in (-1, 1)]:
            i = qi + d
            inside = jnp.logical_and(i >= 0, i < n)
            visits.append((jnp.clip(i, 0, n - 1), 0 if d < 0 else 2,
                           jnp.where(inside, -c * t * abs(d), -NO_TILE)))

    acc_ref[...] = jnp.zeros(acc_ref.shape, _F32)
    l = jnp.zeros((1, 2 * t), _F32)
    bufs = (p_a, p_b)
    for v, (i, side, off) in enumerate(visits):
        if off is None:
            p = jnp.exp2(qk(i, side) + explicit_bias(0) - frame)
        else:
            p = jnp.exp2(qk(i, side) - (frame - off))
        l = l + jnp.sum(p, axis=0, keepdims=True)
        bufs[v % 2][...] = p.astype(_BF16)
        if v > 0:
            acc_ref[...] += pv(visits[v - 1][0], bufs[(v - 1) % 2])
    last = len(visits) - 1
    o_all = (acc_ref[...] + pv(visits[last][0], bufs[last % 2])) / l
    finish(o_all)

    unusable = jnp.logical_or(jnp.logical_not(jnp.sum(o_all * 0.0) == 0.0), jnp.min(l) < MIN_DENOMINATOR)

    @pl.when(unusable)
    def _():
        acc_ref[...] = jnp.zeros(acc_ref.shape, _F32)

        def safe_body(i, carry):
            m, l_run = carry
            sc = qk(i, 1) + explicit_bias(i - qi)
            m_new = jnp.maximum(m, jnp.max(sc, axis=0, keepdims=True))
            alpha = jnp.exp2(m - m_new)
            p = jnp.exp2(sc - m_new)
            p_a[...] = p.astype(_BF16)
            acc_ref[...] = alpha * acc_ref[...] + pv(i, p_a)
            return m_new, alpha * l_run + jnp.sum(p, axis=0, keepdims=True)

        init = (jnp.full((1, 2 * t), -jnp.inf, _F32), jnp.zeros((1, 2 * t), _F32))
        _, l_safe = lax.fori_loop(0, n, safe_body, init)
        finish(acc_ref[...] / l_safe)


def _diff_attention_head(q, k, vt, head, slope, lam_init, lam_params, head_gain, *, tile=512):
    b, s, _ = q.shape
    n = s // tile
    assert n >= 2
    qf, kf = _bias_feature_tables(tile)
    reach = int(math.floor((FLUSH_EXP / (slope * LOG2_E) - 1) / tile)) + 1
    smem = pl.BlockSpec(memory_space=pltpu.SMEM)
    const = lambda shape: pl.BlockSpec(shape, lambda bi, qi: (0,) * len(shape))
    return pl.pallas_call(
        functools.partial(_attn_kernel, slope=slope, reach=reach),
        grid=(b, n),
        in_specs=[smem, const(lam_params.shape),
                  pl.BlockSpec((1, V_DIM), lambda bi, qi: (0, head)),
                  const(qf.shape), const(kf.shape),
                  pl.BlockSpec((1, tile, V_DIM), lambda bi, qi: (bi, qi, head)),
                  pl.BlockSpec((1, s, V_DIM), lambda bi, qi: (bi, 0, head)),
                  pl.BlockSpec((1, V_DIM, s), lambda bi, qi: (bi, head, 0))],
        out_specs=pl.BlockSpec((1, tile, V_DIM), lambda bi, qi: (bi, qi, 0)),
        out_shape=jax.ShapeDtypeStruct((b, s, V_DIM), _BF16),
        scratch_shapes=[pltpu.VMEM((2 * tile, 2 * LANES), _BF16),
                        pltpu.VMEM((V_DIM, 2 * tile), _F32),
                        pltpu.VMEM((tile, 2 * tile), _BF16), pltpu.VMEM((tile, 2 * tile), _BF16),
                        pltpu.SMEM((1,), _F32)],
        compiler_params=_params("arbitrary", "arbitrary"),
        name=f"diff_attn_h{head}",
    )(lam_init, lam_params, head_gain.reshape(1, ATTN_WIDTH), qf, kf, q, k, vt)


def _mix_out_ffn_kernel(x_ref, a_ref, a_prev_ref, a_next_ref, pw_ref, ps_ref, yf_ref, *rest, ff_chunk, final):
    head_refs, rest = rest[:ATTN_HEADS], rest[ATTN_HEADS:]
    if final:
        wo_ref, g_ref, wg_ref, wu_ref, wd_ref, gf_ref, o_ref, buf_ref = rest
    else:
        wo_ref, g_ref, wg_ref, wu_ref, wd_ref, o_ref, buf_ref = rest
    i = pl.program_id(1)
    tm = a_ref.shape[1]
    s_len = tm * pl.num_programs(1)
    halo = POOL_HALO
    buf_ref[0:halo] = jnp.where(i > 0, a_prev_ref[0], 0.0)
    buf_ref[halo:halo + tm] = a_ref[0]
    buf_ref[halo + tm:] = jnp.where(i < pl.num_programs(1) - 1, a_next_ref[0], 0.0)
    lane = lax.broadcasted_iota(jnp.int32, (1, LANES), 1)
    pick = lambda v0, v1: jnp.where(lane >= GROUP_DIM, v1, v0)

    def pooled(lanes, windows):
        shifted = lambda d: buf_ref[halo + d:halo + d + tm, lanes]
        a = shifted(0)
        sums, total, lo, hi = [], a, 0, 0
        for w in windows:
            left, right = w // 2, w - 1 - w // 2
            for d in list(range(-left, lo)) + list(range(hi + 1, right + 1)):
                total = total + shifted(d)
            lo, hi = -left, right
            sums.append(total)
        win = pick(*sums)
        mixed = win * pick(*[1.0 / w for w in windows]) - a

        def edge(rows):
            t = i * tm + rows.start + lax.broadcasted_iota(jnp.int32, (halo, LANES), 0)
            left, right = pick(*[w // 2 for w in windows]), pick(*[w - 1 - w // 2 for w in windows])
            cnt = jnp.minimum(t + right + 1, s_len) - jnp.maximum(t - left, 0)
            return win[rows] / cnt.astype(_F32) - a[rows]

        return jnp.concatenate([edge(slice(0, halo)), mixed[halo:tm - halo], edge(slice(tm - halo, tm))], axis=0)

    mixed = jnp.concatenate([pooled(slice(0, LANES), POOL_WINDOWS[:2]),
                             pooled(slice(LANES, 2 * LANES), POOL_WINDOWS[2:])], axis=1)
    y_pool = jnp.dot(mixed.astype(_BF16), pw_ref[...], preferred_element_type=_F32) * ps_ref[...]

    rest = jnp.concatenate([yf_ref[0].astype(_BF16)] + [r[0] for r in head_refs], axis=1)
    y = jnp.dot(rest, wo_ref[POOL_WIDTH:, :], preferred_element_type=_F32)
    y = y + jnp.dot(y_pool.astype(_BF16), wo_ref[:POOL_WIDTH, :], preferred_element_type=_F32)
    x = _swiglu_half_step(x_ref[0] + y, g_ref, wg_ref, wu_ref, wd_ref, ff_chunk)
    o_ref[0] = _rms_scale(x, gf_ref[...]) if final else x


def _mix_out_ffn(x, a, pool_bd, pool_scale, y_four, y_heads, w_out, g, wg, wu, wd, layer, final_g=None,
                 *, tm=512, ff_chunk=256):
    b, s, d = x.shape
    d_ff = wg.shape[2]
    hb = tm // POOL_HALO
    n_halo = s // POOL_HALO
    row = lambda bi, i: (bi, i, 0)
    const = lambda bi, i: (0, 0)
    resident = lambda shape: pl.BlockSpec((None,) + shape, lambda bi, i: (layer, 0, 0),
                                          pipeline_mode=pl.Buffered(1))
    in_specs = ([pl.BlockSpec((1, tm, d), row),
                 pl.BlockSpec((1, tm, POOL_WIDTH), row),
                 pl.BlockSpec((1, POOL_HALO, POOL_WIDTH), lambda bi, i: (bi, jnp.maximum(i * hb - 1, 0), 0)),
                 pl.BlockSpec((1, POOL_HALO, POOL_WIDTH),
                              lambda bi, i: (bi, jnp.minimum((i + 1) * hb, n_halo - 1), 0)),
                 pl.BlockSpec(pool_bd.shape, const), pl.BlockSpec((1, POOL_WIDTH), const),
                 pl.BlockSpec((1, tm, FOURIER_WIDTH), row)]
                + [pl.BlockSpec((1, tm, V_DIM), row)] * ATTN_HEADS
                + [resident((d, d)), pl.BlockSpec((1, d), const),
                   resident((d, d_ff)), resident((d, d_ff)), resident((d_ff, d))])
    args = [x, a, a, a, pool_bd, pool_scale.reshape(1, POOL_WIDTH), y_four, *y_heads, w_out,
            g.reshape(1, d), wg, wu, wd]
    if final_g is not None:
        in_specs.append(pl.BlockSpec((1, d), const))
        args.append(final_g.reshape(1, d))
    return pl.pallas_call(
        functools.partial(_mix_out_ffn_kernel, ff_chunk=ff_chunk, final=final_g is not None),
        grid=(b, s // tm),
        in_specs=in_specs,
        out_specs=pl.BlockSpec((1, tm, d), row),
        out_shape=jax.ShapeDtypeStruct((b, s, d), _F32),
        scratch_shapes=[pltpu.VMEM((tm + 2 * POOL_HALO, POOL_WIDTH), _F32)],
        compiler_params=_params("parallel", "parallel"),
        name="mix_out_ffn",
    )(*args)


def kernel(x, ffn1_norm, ffn1_w_gate, ffn1_w_up, ffn1_w_down, mix_norm, w_in, pool_w, pool_scale, fourier_w, lam_q1, lam_k1, lam_q2, lam_k2, attn_head_norm, w_out, ffn2_norm, ffn2_w_gate, ffn2_w_up, ffn2_w_down, final_norm):
    b, s, d = x.shape
    depth = w_in.shape[0]
    bf = lambda w: w.astype(_BF16)
    (chan_hi, chan_lo), (m1_hi, m1_lo), (m2_hi, m2_lo), tw_cos, tw_sin = _dft_tables(s)
    alibi = [2.0 ** (-8.0 * (i + 1) / ATTN_HEADS) for i in range(ATTN_HEADS)]
    assert all(math.frexp(v)[0] == 0.5 for v in alibi), "position features rely on power-of-two slopes"
    assert POOL_WIDTH + FOURIER_WIDTH == ATTN_WIDTH
    assert all(w & (w - 1) == 0 and w <= 2 * POOL_HALO for w in POOL_WINDOWS) and POOL_WIDTH == 2 * LANES
    ffn1 = (bf(ffn1_w_gate), bf(ffn1_w_up), bf(ffn1_w_down))
    ffn2 = (bf(ffn2_w_gate), bf(ffn2_w_up), bf(ffn2_w_down))
    w_in_b, w_out_b, fourier_b = bf(w_in), bf(w_out), bf(fourier_w)
    w_vt_b = jnp.swapaxes(w_in_b[:, :, 3 * ATTN_WIDTH:], 1, 2)

    for l in range(depth):
        x, a, z, q, k, vt = _ffn_mix_in(x, ffn1_norm[l], *ffn1, mix_norm[l], w_in_b, w_vt_b, l, chan_hi, chan_lo)
        y_four = _dft_stage2(_dft_stage1(z, m1_hi, m1_lo, tw_cos, tw_sin), m2_hi, m2_lo, fourier_b, l)
        lam_init = jnp.full((1,), 0.8 - 0.6 * math.exp(-0.3 * l), _F32)
        lam_params = jnp.stack([lam_q1[l], lam_k1[l], lam_q2[l], lam_k2[l]]).astype(_F32)
        y_heads = [_diff_attention_head(q, k, vt, h, alibi[h], lam_init, lam_params, attn_head_norm[l])
                   for h in range(ATTN_HEADS)]
        pool_bd = jax.scipy.linalg.block_diag(*[pool_w[l, g] for g in range(len(POOL_WINDOWS))])
        x = _mix_out_ffn(x, a, bf(pool_bd), pool_scale[l], y_four, y_heads, w_out_b, ffn2_norm[l], *ffn2, l,
                         final_norm if l == depth - 1 else None)
    return x
```

```python
import functools
import math

import numpy as np
import jax
import jax.numpy as jnp
from jax import lax
from jax.experimental import pallas as pl
from jax.experimental.pallas import tpu as pltpu

_F32 = jnp.float32
_BF16 = jnp.bfloat16

NORM_EPS = 1e-6
POOL_WIDTH = 256
FOURIER_WIDTH = 256
ATTN_WIDTH = 512
POOL_WINDOWS = (2, 4, 8, 16)
GROUP_DIM = 64
ATTN_HEADS = 4
HEAD_DIM = 64
V_DIM = 2 * HEAD_DIM
POOL_HALO = 8
DFT_L = 128
LANES = 128
LOG2_E = math.log2(math.e)
FLUSH_EXP = 127.0
FRAME_SLACK = 1.0 + 2.0 ** -5
NO_TILE = 1e30
MIN_DENOMINATOR = 2.0 ** -90

_VMEM_LIMIT = 56 * 1024 * 1024


def _params(*semantics, flags=None):
    return pltpu.CompilerParams(dimension_semantics=semantics, vmem_limit_bytes=_VMEM_LIMIT, flags=flags)


def _rms_scale(x, g):
    ms = jnp.mean(x * x, axis=-1, keepdims=True)
    return x * lax.rsqrt(ms + NORM_EPS) * g


def _split_hi_lo(x):
    hi = x.astype(_BF16)
    lo = (x - hi.astype(_F32)).astype(_BF16)
    return hi, lo


def _dot3(a_hi, a_lo, b_hi, b_lo):
    d = functools.partial(jnp.dot, preferred_element_type=_F32)
    return d(a_hi, b_hi) + (d(a_lo, b_hi) + d(a_hi, b_lo))


def _swiglu_half_step(x, g_ref, wg_ref, wu_ref, wd_ref, ff_chunk):
    hn = _rms_scale(x, g_ref[...]).astype(_BF16)
    acc = jnp.zeros(x.shape, _F32)
    for c in range(wg_ref.shape[1] // ff_chunk):
        sl = slice(c * ff_chunk, (c + 1) * ff_chunk)
        gate = jnp.dot(hn, wg_ref[:, sl], preferred_element_type=_F32)
        up = jnp.dot(hn, wu_ref[:, sl], preferred_element_type=_F32)
        act = (gate / (1.0 + jnp.exp(-gate)) * up).astype(_BF16)
        acc = acc + jnp.dot(act, wd_ref[sl, :], preferred_element_type=_F32)
    return x + 0.5 * acc


def _ffn_mix_in_kernel(x_ref, g1_ref, wg_ref, wu_ref, wd_ref, g_ref, waf_ref, wq_ref, wk_ref, wvt_ref,
                       chi_ref, clo_ref, x_out_ref, a_ref, z_ref, q_ref, k_ref, vt_ref, *, ff_chunk):
    x = _swiglu_half_step(x_ref[0], g1_ref, wg_ref, wu_ref, wd_ref, ff_chunk)
    x_out_ref[0] = x
    hn = _rms_scale(x, g_ref[...]).astype(_BF16)
    af = jnp.dot(hn, waf_ref[...], preferred_element_type=_F32)
    a_ref[0] = af[:, :POOL_WIDTH]
    f_hi, f_lo = _split_hi_lo(af[:, POOL_WIDTH:])
    z_ref[0] = _dot3(f_hi, f_lo, chi_ref[...], clo_ref[...])
    q = jnp.dot(hn, wq_ref[...], preferred_element_type=_F32)
    q_ref[0] = (q * (HEAD_DIM ** -0.5 * LOG2_E)).astype(_BF16)
    k_ref[0] = jnp.dot(hn, wk_ref[...], preferred_element_type=_F32).astype(_BF16)
    vt = lax.dot_general(wvt_ref[...], hn, (((1,), (1,)), ((), ())), preferred_element_type=_F32)
    vt_ref[0] = vt.astype(_BF16)


def _ffn_mix_in(x, g1, wg, wu, wd, g, w_in, w_vt, layer, chan_hi, chan_lo, *, tm=512, ff_chunk=256):
    b, s, d = x.shape
    d_ff = wg.shape[2]
    const = lambda bi, i: (0, 0)
    row = lambda bi, i: (bi, i, 0)
    full = lambda arr: pl.BlockSpec(arr.shape, const)
    once = pl.Buffered(1)
    resident = lambda shape: pl.BlockSpec((None,) + shape, lambda bi, i: (layer, 0, 0), pipeline_mode=once)
    cols = lambda j: pl.BlockSpec((None, d, ATTN_WIDTH), lambda bi, i: (layer, 0, j), pipeline_mode=once)
    return pl.pallas_call(
        functools.partial(_ffn_mix_in_kernel, ff_chunk=ff_chunk),
        grid=(b, s // tm),
        in_specs=[pl.BlockSpec((1, tm, d), row), pl.BlockSpec((1, d), const),
                  resident((d, d_ff)), resident((d, d_ff)), resident((d_ff, d)),
                  pl.BlockSpec((1, d), const), cols(0), cols(1), cols(2), resident(w_vt.shape[1:]),
                  full(chan_hi), full(chan_lo)],
        out_specs=[pl.BlockSpec((1, tm, d), row),
                   pl.BlockSpec((1, tm, POOL_WIDTH), row),
                   pl.BlockSpec((1, tm, 2 * FOURIER_WIDTH), row),
                   pl.BlockSpec((1, tm, ATTN_WIDTH), row),
                   pl.BlockSpec((1, tm, ATTN_WIDTH), row),
                   pl.BlockSpec((1, ATTN_WIDTH, tm), lambda bi, i: (bi, 0, i))],
        out_shape=[jax.ShapeDtypeStruct((b, s, d), _F32),
                   jax.ShapeDtypeStruct((b, s, POOL_WIDTH), _F32),
                   jax.ShapeDtypeStruct((b, s, 2 * FOURIER_WIDTH), _F32),
                   jax.ShapeDtypeStruct((b, s, ATTN_WIDTH), _BF16),
                   jax.ShapeDtypeStruct((b, s, ATTN_WIDTH), _BF16),
                   jax.ShapeDtypeStruct((b, ATTN_WIDTH, s), _BF16)],
        compiler_params=_params("parallel", "parallel"),
        name="ffn_mix_in",
    )(x, g1.reshape(1, d), wg, wu, wd, g.reshape(1, d), w_in, w_in, w_in, w_vt, chan_hi, chan_lo)


def _dft_tables(s):
    h, l = s // DFT_L, DFT_L
    two_pi = 2.0 * np.pi
    c = np.arange(GROUP_DIM)
    ang = two_pi * np.outer(c, c) / GROUP_DIM
    eye = np.eye(FOURIER_WIDTH // GROUP_DIM)
    chan = np.concatenate([np.kron(eye, np.cos(ang)), np.kron(eye, -np.sin(ang))], axis=1)
    u = np.arange(h)
    ang1 = two_pi * np.outer(u, u) / h
    stage1 = np.concatenate([np.cos(ang1), -np.sin(ang1)], axis=0)
    t = np.arange(l)
    angt = two_pi * np.outer(u, t) / s
    tw_cos, tw_sin = np.cos(angt), np.sin(angt)
    ang2 = two_pi * np.outer(t, t) / l
    scale = 1.0 / math.sqrt(s * GROUP_DIM)
    stage2 = np.concatenate([np.cos(ang2), np.sin(ang2)], axis=1) * scale
    as_f32 = lambda a: jnp.asarray(a, _F32)
    return (_split_hi_lo(as_f32(chan)), _split_hi_lo(as_f32(stage1)), _split_hi_lo(as_f32(stage2)),
            jnp.repeat(as_f32(tw_cos), LANES, axis=1), jnp.repeat(as_f32(tw_sin), LANES, axis=1))


def _dft_stage1_kernel(z_ref, mhi_ref, mlo_ref, twc_ref, tws_ref, o_ref, *, slabs):
    h = z_ref.shape[1]
    w = FOURIER_WIDTH
    z = jnp.concatenate([z_ref[0, :, j, :] for j in range(slabs)], axis=1)
    z_hi, z_lo = _split_hi_lo(z)
    p = _dot3(mhi_ref[...], mlo_ref[...], z_hi, z_lo)
    for j in range(slabs):
        re, im = slice(2 * j * w, (2 * j + 1) * w), slice((2 * j + 1) * w, (2 * j + 2) * w)
        ar = p[:h, re] - p[h:, im]
        ai = p[:h, im] + p[h:, re]
        tw = slice(j * LANES, (j + 1) * LANES)
        tc = jnp.concatenate([twc_ref[:, tw]] * (w // LANES), axis=1)
        ts = jnp.concatenate([tws_ref[:, tw]] * (w // LANES), axis=1)
        o_ref[0, :, j, :] = jnp.concatenate([ar * tc + ai * ts, ai * tc - ar * ts], axis=1)


def _dft_stage1(z, m_hi, m_lo, tw_cos, tw_sin, *, slabs=16):
    b, s, zw = z.shape
    h = s // DFT_L
    z4 = z.reshape(b, h, DFT_L, zw)
    const = lambda bi, j: (0, 0)
    slab_block = pl.BlockSpec((1, h, slabs, zw), lambda bi, j: (bi, 0, j, 0))
    return pl.pallas_call(
        functools.partial(_dft_stage1_kernel, slabs=slabs),
        grid=(b, DFT_L // slabs),
        in_specs=[slab_block,
                  pl.BlockSpec(m_hi.shape, const), pl.BlockSpec(m_lo.shape, const),
                  pl.BlockSpec((h, slabs * LANES), lambda bi, j: (0, j)),
                  pl.BlockSpec((h, slabs * LANES), lambda bi, j: (0, j))],
        out_specs=slab_block,
        out_shape=jax.ShapeDtypeStruct(z4.shape, _F32),
        compiler_params=_params("parallel", "parallel"),
        name="dft_stage1",
    )(z4, m_hi, m_lo, tw_cos, tw_sin)


def _dft_stage2_kernel(a_ref, mhi_ref, mlo_ref, fw_ref, o_ref, *, rows):
    w = FOURIER_WIDTH
    for r in range(rows):
        blk = a_ref[0, r]
        stacked = jnp.concatenate([blk[:, :w], blk[:, w:]], axis=0)
        s_hi, s_lo = _split_hi_lo(stacked)
        y = _dot3(mhi_ref[...], mlo_ref[...], s_hi, s_lo)
        o_ref[0, :, r, :] = jnp.dot(y.astype(_BF16), fw_ref[...], preferred_element_type=_F32)


def _dft_stage2(a4, m_hi, m_lo, fourier_w, layer, *, rows=8):
    b, h, l, zw = a4.shape
    w = FOURIER_WIDTH
    const = lambda bi, j: (0, 0)
    out = pl.pallas_call(
        functools.partial(_dft_stage2_kernel, rows=rows),
        grid=(b, h // rows),
        in_specs=[pl.BlockSpec((1, rows, l, zw), lambda bi, j: (bi, j, 0, 0)),
                  pl.BlockSpec(m_hi.shape, const), pl.BlockSpec(m_lo.shape, const),
                  pl.BlockSpec((None,) + fourier_w.shape[1:], lambda bi, j: (layer, 0, 0))],
        out_specs=pl.BlockSpec((1, l, rows, w), lambda bi, j: (bi, 0, j, 0)),
        out_shape=jax.ShapeDtypeStruct((b, l, h, w), _F32),
        compiler_params=_params("parallel", "parallel"),
        name="dft_stage2",
    )(a4, m_hi, m_lo, fourier_w)
    return out.reshape(b, l * h, w)


def _bias_feature_tables(tile):
    pos = np.arange(tile)
    hi, lo = (pos // LANES) * LANES, pos % LANES
    qf = np.zeros((tile, LANES), np.float32)
    kf = np.zeros((tile, LANES), np.float32)
    rest = LOG2_E
    for p in range(3):
        part = float(np.float32(rest).astype(jnp.bfloat16).astype(np.float32))
        rest -= part
        qf[:, 2 * p], qf[:, 2 * p + 1] = hi, lo
        kf[:, 2 * p], kf[:, 2 * p + 1] = part, part
        qf[:, 6 + 2 * p], qf[:, 7 + 2 * p] = -part, -part
        kf[:, 6 + 2 * p], kf[:, 7 + 2 * p] = hi, lo
    return jnp.asarray(qf, _BF16), jnp.asarray(np.stack([-kf, 0.0 * kf, kf]), _BF16)


def _attn_kernel(lam_init_ref, lam_ref, gain_ref, qf_ref, kf_ref, q_ref, k_ref, vt_ref, o_ref,
                 w_ref, acc_ref, p_ref, knorm_ref, *, slope, reach, q_tiles):
    t = q_ref.shape[1] // q_tiles
    n = k_ref.shape[1] // t
    step = pl.program_id(1)
    c = slope * LOG2_E
    nt = (((1,), (1,)), ((), ()))
    half_rows = (lax.broadcasted_iota(jnp.int32, (8, LANES), 1) // HEAD_DIM
                 == lax.broadcasted_iota(jnp.int32, (8, LANES), 0)).astype(_F32)

    def half_norms(x):
        sq = jnp.square(x.astype(_F32))
        return jnp.sqrt(lax.dot_general(half_rows, sq, nt, preferred_element_type=_F32)[0:2])

    @pl.when(step == 0)
    def _():
        knorm_ref[0] = jnp.max(half_norms(k_ref[0]))

    def explicit_bias(tile_delta):
        dist = jnp.abs(lax.broadcasted_iota(jnp.int32, (t, t), 0) - lax.broadcasted_iota(jnp.int32, (t, t), 1)
                       + tile_delta * t)
        bias = dist.astype(_F32) * (-c)
        return jnp.concatenate([bias, bias], axis=1)

    def pv(tile_idx, p_buf):
        k0 = pl.multiple_of(tile_idx * t, t)
        return jnp.dot(vt_ref[0, :, pl.ds(k0, t)], p_buf[...], preferred_element_type=_F32)

    def query_tile(u):
        qi = step * q_tiles + u
        rows = slice(u * t, (u + 1) * t)
        w_u, acc_u, p_a, p_b = w_ref.at[u], acc_ref.at[u], p_ref.at[2 * u], p_ref.at[2 * u + 1]

        qh = q_ref[0, rows, :]
        lane = lax.broadcasted_iota(jnp.int32, qh.shape, 1)
        zero = jnp.zeros_like(qh)
        halves = jnp.concatenate([jnp.where(lane < HEAD_DIM, qh, zero), jnp.where(lane >= HEAD_DIM, qh, zero)],
                                 axis=0)
        qf = (qf_ref[...].astype(_F32) * slope).astype(_BF16)
        w_u[...] = jnp.concatenate([halves, jnp.concatenate([qf, qf], axis=0)], axis=1)
        qn = half_norms(qh)
        frame = jnp.concatenate([qn[0:1], qn[1:2]], axis=1) * (knorm_ref[0] * FRAME_SLACK) + 0.5

        def qk(tile_idx, side):
            k0 = pl.multiple_of(tile_idx * t, t)
            lhs = jnp.concatenate([k_ref[0, pl.ds(k0, t), :], kf_ref[side]], axis=1)
            return lax.dot_general(lhs, w_u[...], nt, preferred_element_type=_F32)

        def finish(o_all):
            lam_init = lam_init_ref[0]
            lp = lam_ref[...]
            lam = (jnp.exp(jnp.sum(lp[0:1] * lp[1:2], axis=-1, keepdims=True))
                   - jnp.exp(jnp.sum(lp[2:3] * lp[3:4], axis=-1, keepdims=True)) + lam_init)
            o = (o_all[:, :t] - lam * o_all[:, t:]).T
            o = _rms_scale(o, gain_ref[...]) * (1.0 - lam_init)
            o_ref[0, rows, :] = o.astype(o_ref.dtype)

        if 2 * reach + 1 >= n:
            others = [j + (qi <= j).astype(jnp.int32) for j in range(n - 1)]
            visits = [(qi, 1, None)] + [(i, jnp.where(i < qi, 0, 2), jnp.abs(i - qi).astype(_F32) * (-c * t))
                                        for i in others]
        else:
            visits = [(qi, 1, None)]
            for d in [sign * dist for dist in range(1, reach + 1) for sign in (-1, 1)]:
                i = qi + d
                inside = jnp.logical_and(i >= 0, i < n)
                visits.append((jnp.clip(i, 0, n - 1), 0 if d < 0 else 2,
                               jnp.where(inside, -c * t * abs(d), -NO_TILE)))

        acc_u[...] = jnp.zeros(acc_u.shape, _F32)
        l = jnp.zeros((1, 2 * t), _F32)
        bufs = (p_a, p_b)
        for v, (i, side, off) in enumerate(visits):
            if off is None:
                p = jnp.exp2(qk(i, side) + explicit_bias(0) - frame)
            else:
                p = jnp.exp2(qk(i, side) - (frame - off))
            l = l + jnp.sum(p, axis=0, keepdims=True)
            bufs[v % 2][...] = p.astype(_BF16)
            if v > 0:
                acc_u[...] += pv(visits[v - 1][0], bufs[(v - 1) % 2])
        last = len(visits) - 1
        o_all = (acc_u[...] + pv(visits[last][0], bufs[last % 2])) / l
        finish(o_all)
        unusable = jnp.logical_or(jnp.logical_not(jnp.sum(o_all * 0.0) == 0.0), jnp.min(l) < MIN_DENOMINATOR)

        def recompute():
            acc_u[...] = jnp.zeros(acc_u.shape, _F32)

            def safe_body(i, carry):
                m, l_run = carry
                sc = qk(i, 1) + explicit_bias(i - qi)
                m_new = jnp.maximum(m, jnp.max(sc, axis=0, keepdims=True))
                alpha = jnp.exp2(m - m_new)
                p = jnp.exp2(sc - m_new)
                p_a[...] = p.astype(_BF16)
                acc_u[...] = alpha * acc_u[...] + pv(i, p_a)
                return m_new, alpha * l_run + jnp.sum(p, axis=0, keepdims=True)

            init = (jnp.full((1, 2 * t), -jnp.inf, _F32), jnp.zeros((1, 2 * t), _F32))
            _, l_safe = lax.fori_loop(0, n, safe_body, init)
            finish(acc_u[...] / l_safe)

        return unusable, recompute

    for unusable, recompute in [query_tile(u) for u in range(q_tiles)]:
        pl.when(unusable)(recompute)


def _diff_attention_head(q, k, vt, head, slope, lam_init, lam_params, head_gain, *, tile=512):
    b, s, _ = q.shape
    n = s // tile
    qf, kf = _bias_feature_tables(tile)
    reach = int(math.floor((FLUSH_EXP / (slope * LOG2_E) - 1) / tile)) + 1
    q_tiles = 2 if 2 * reach + 1 >= n else 4
    assert n >= 2 and n % q_tiles == 0
    smem = pl.BlockSpec(memory_space=pltpu.SMEM)
    const = lambda shape: pl.BlockSpec(shape, lambda bi, qi: (0,) * len(shape))
    return pl.pallas_call(
        functools.partial(_attn_kernel, slope=slope, reach=reach, q_tiles=q_tiles),
        grid=(b, n // q_tiles),
        in_specs=[smem, const(lam_params.shape),
                  pl.BlockSpec((1, V_DIM), lambda bi, qi: (0, head)),
                  const(qf.shape), const(kf.shape),
                  pl.BlockSpec((1, q_tiles * tile, V_DIM), lambda bi, qi: (bi, qi, head)),
                  pl.BlockSpec((1, s, V_DIM), lambda bi, qi: (bi, 0, head)),
                  pl.BlockSpec((1, V_DIM, s), lambda bi, qi: (bi, head, 0))],
        out_specs=pl.BlockSpec((1, q_tiles * tile, V_DIM), lambda bi, qi: (bi, qi, 0)),
        out_shape=jax.ShapeDtypeStruct((b, s, V_DIM), _BF16),
        scratch_shapes=[pltpu.VMEM((q_tiles, 2 * tile, 2 * LANES), _BF16),
                        pltpu.VMEM((q_tiles, V_DIM, 2 * tile), _F32),
                        pltpu.VMEM((2 * q_tiles, tile, 2 * tile), _BF16),
                        pltpu.SMEM((1,), _F32)],
        compiler_params=_params("arbitrary", "arbitrary"),
        name=f"diff_attn_h{head}",
    )(lam_init, lam_params, head_gain.reshape(1, ATTN_WIDTH), qf, kf, q, k, vt)


def _mix_out_ffn_kernel(x_ref, a_ref, a_prev_ref, a_next_ref, pw_ref, ps_ref, yf_ref, *rest, ff_chunk, final):
    head_refs, rest = rest[:ATTN_HEADS], rest[ATTN_HEADS:]
    if final:
        wo_ref, g_ref, wg_ref, wu_ref, wd_ref, gf_ref, o_ref, buf_ref = rest
    else:
        wo_ref, g_ref, wg_ref, wu_ref, wd_ref, o_ref, buf_ref = rest
    i = pl.program_id(1)
    tm = a_ref.shape[1]
    s_len = tm * pl.num_programs(1)
    halo = POOL_HALO
    buf_ref[0:halo] = jnp.where(i > 0, a_prev_ref[0], 0.0)
    buf_ref[halo:halo + tm] = a_ref[0]
    buf_ref[halo + tm:] = jnp.where(i < pl.num_programs(1) - 1, a_next_ref[0], 0.0)
    lane = lax.broadcasted_iota(jnp.int32, (1, LANES), 1)
    pick = lambda v0, v1: jnp.where(lane >= GROUP_DIM, v1, v0)

    def pooled(lanes, windows):
        shifted = lambda d: buf_ref[halo + d:halo + d + tm, lanes]
        a = shifted(0)
        sums, total, lo, hi = [], a, 0, 0
        for w in windows:
            left, right = w // 2, w - 1 - w // 2
            for d in list(range(-left, lo)) + list(range(hi + 1, right + 1)):
                total = total + shifted(d)
            lo, hi = -left, right
            sums.append(total)
        win = pick(*sums)
        mixed = win * pick(*[1.0 / w for w in windows]) - a

        def edge(rows):
            t = i * tm + rows.start + lax.broadcasted_iota(jnp.int32, (halo, LANES), 0)
            left, right = pick(*[w // 2 for w in windows]), pick(*[w - 1 - w // 2 for w in windows])
            cnt = jnp.minimum(t + right + 1, s_len) - jnp.maximum(t - left, 0)
            return win[rows] / cnt.astype(_F32) - a[rows]

        return jnp.concatenate([edge(slice(0, halo)), mixed[halo:tm - halo], edge(slice(tm - halo, tm))], axis=0)

    mixed = jnp.concatenate([pooled(slice(0, LANES), POOL_WINDOWS[:2]),
                             pooled(slice(LANES, 2 * LANES), POOL_WINDOWS[2:])], axis=1)
    y_pool = jnp.dot(mixed.astype(_BF16), pw_ref[...], preferred_element_type=_F32) * ps_ref[...]

    rest = jnp.concatenate([yf_ref[0].astype(_BF16)] + [r[0] for r in head_refs], axis=1)
    y = jnp.dot(rest, wo_ref[POOL_WIDTH:, :], preferred_element_type=_F32)
    y = y + jnp.dot(y_pool.astype(_BF16), wo_ref[:POOL_WIDTH, :], preferred_element_type=_F32)
    x = _swiglu_half_step(x_ref[0] + y, g_ref, wg_ref, wu_ref, wd_ref, ff_chunk)
    o_ref[0] = _rms_scale(x, gf_ref[...]) if final else x


def _mix_out_ffn(x, a, pool_bd, pool_scale, y_four, y_heads, w_out, g, wg, wu, wd, layer, final_g=None,
                 *, tm=512, ff_chunk=256):
    b, s, d = x.shape
    d_ff = wg.shape[2]
    hb = tm // POOL_HALO
    n_halo = s // POOL_HALO
    row = lambda bi, i: (bi, i, 0)
    const = lambda bi, i: (0, 0)
    resident = lambda shape: pl.BlockSpec((None,) + shape, lambda bi, i: (layer, 0, 0),
                                          pipeline_mode=pl.Buffered(1))
    in_specs = ([pl.BlockSpec((1, tm, d), row),
                 pl.BlockSpec((1, tm, POOL_WIDTH), row),
                 pl.BlockSpec((1, POOL_HALO, POOL_WIDTH), lambda bi, i: (bi, jnp.maximum(i * hb - 1, 0), 0)),
                 pl.BlockSpec((1, POOL_HALO, POOL_WIDTH),
                              lambda bi, i: (bi, jnp.minimum((i + 1) * hb, n_halo - 1), 0)),
                 pl.BlockSpec(pool_bd.shape, const), pl.BlockSpec((1, POOL_WIDTH), const),
                 pl.BlockSpec((1, tm, FOURIER_WIDTH), row)]
                + [pl.BlockSpec((1, tm, V_DIM), row)] * ATTN_HEADS
                + [resident((d, d)), pl.BlockSpec((1, d), const),
                   resident((d, d_ff)), resident((d, d_ff)), resident((d_ff, d))])
    args = [x, a, a, a, pool_bd, pool_scale.reshape(1, POOL_WIDTH), y_four, *y_heads, w_out,
            g.reshape(1, d), wg, wu, wd]
    if final_g is not None:
        in_specs.append(pl.BlockSpec((1, d), const))
        args.append(final_g.reshape(1, d))
    return pl.pallas_call(
        functools.partial(_mix_out_ffn_kernel, ff_chunk=ff_chunk, final=final_g is not None),
        grid=(b, s // tm),
        in_specs=in_specs,
        out_specs=pl.BlockSpec((1, tm, d), row),
        out_shape=jax.ShapeDtypeStruct((b, s, d), _F32),
        scratch_shapes=[pltpu.VMEM((tm + 2 * POOL_HALO, POOL_WIDTH), _F32)],
        compiler_params=_params("parallel", "parallel"),
        name="mix_out_ffn",
    )(*args)


def kernel(x, ffn1_norm, ffn1_w_gate, ffn1_w_up, ffn1_w_down, mix_norm, w_in, pool_w, pool_scale, fourier_w, lam_q1, lam_k1, lam_q2, lam_k2, attn_head_norm, w_out, ffn2_norm, ffn2_w_gate, ffn2_w_up, ffn2_w_down, final_norm):
    b, s, d = x.shape
    depth = w_in.shape[0]
    bf = lambda w: w.astype(_BF16)
    (chan_hi, chan_lo), (m1_hi, m1_lo), (m2_hi, m2_lo), tw_cos, tw_sin = _dft_tables(s)
    alibi = [2.0 ** (-8.0 * (i + 1) / ATTN_HEADS) for i in range(ATTN_HEADS)]
    assert all(math.frexp(v)[0] == 0.5 for v in alibi), "position features rely on power-of-two slopes"
    assert POOL_WIDTH + FOURIER_WIDTH == ATTN_WIDTH
    assert all(w & (w - 1) == 0 and w <= 2 * POOL_HALO for w in POOL_WINDOWS) and POOL_WIDTH == 2 * LANES
    ffn1 = (bf(ffn1_w_gate), bf(ffn1_w_up), bf(ffn1_w_down))
    ffn2 = (bf(ffn2_w_gate), bf(ffn2_w_up), bf(ffn2_w_down))
    w_in_b, w_out_b, fourier_b = bf(w_in), bf(w_out), bf(fourier_w)
    w_vt_b = jnp.swapaxes(w_in_b[:, :, 3 * ATTN_WIDTH:], 1, 2)

    for l in range(depth):
        x, a, z, q, k, vt = _ffn_mix_in(x, ffn1_norm[l], *ffn1, mix_norm[l], w_in_b, w_vt_b, l, chan_hi, chan_lo)
        y_four = _dft_stage2(_dft_stage1(z, m1_hi, m1_lo, tw_cos, tw_sin), m2_hi, m2_lo, fourier_b, l)
        lam_init = jnp.full((1,), 0.8 - 0.6 * math.exp(-0.3 * l), _F32)
        lam_params = jnp.stack([lam_q1[l], lam_k1[l], lam_q2[l], lam_k2[l]]).astype(_F32)
        y_heads = [_diff_attention_head(q, k, vt, h, alibi[h], lam_init, lam_params, attn_head_norm[l])
                   for h in range(ATTN_HEADS)]
        pool_bd = jax.scipy.linalg.block_diag(*[pool_w[l, g] for g in range(len(POOL_WINDOWS))])
        x = _mix_out_ffn(x, a, bf(pool_bd), pool_scale[l], y_four, y_heads, w_out_b, ffn2_norm[l], *ffn2, l,
                         final_norm if l == depth - 1 else None)
    return x
```

```python
import functools
import math

import numpy as np
import jax
import jax.numpy as jnp
from jax import lax
from jax.experimental import pallas as pl
from jax.experimental.pallas import tpu as pltpu

_F32 = jnp.float32
_BF16 = jnp.bfloat16

NORM_EPS = 1e-6
POOL_WIDTH = 256
FOURIER_WIDTH = 256
ATTN_WIDTH = 512
POOL_WINDOWS = (2, 4, 8, 16)
GROUP_DIM = 64
ATTN_HEADS = 4
HEAD_DIM = 64
V_DIM = 2 * HEAD_DIM
POOL_HALO = 8
DFT_L = 128
LANES = 128
LOG2_E = math.log2(math.e)
FLUSH_EXP = 127.0
FRAME_SLACK = 1.0 + 2.0 ** -5
NO_TILE = 1e30
MIN_DENOMINATOR = 2.0 ** -90

_VMEM_LIMIT = 56 * 1024 * 1024


def _params(*semantics, flags=None):
    return pltpu.CompilerParams(dimension_semantics=semantics, vmem_limit_bytes=_VMEM_LIMIT, flags=flags)


def _rms_scale(x, g):
    ms = jnp.mean(x * x, axis=-1, keepdims=True)
    return x * lax.rsqrt(ms + NORM_EPS) * g


def _split_hi_lo(x):
    hi = x.astype(_BF16)
    lo = (x - hi.astype(_F32)).astype(_BF16)
    return hi, lo


def _dot3(a_hi, a_lo, b_hi, b_lo):
    d = functools.partial(jnp.dot, preferred_element_type=_F32)
    return d(a_hi, b_hi) + (d(a_lo, b_hi) + d(a_hi, b_lo))


def _swiglu_half_step(x, g_ref, wg_ref, wu_ref, wd_ref, ff_chunk):
    hn = _rms_scale(x, g_ref[...]).astype(_BF16)
    acc = jnp.zeros(x.shape, _F32)
    for c in range(wg_ref.shape[1] // ff_chunk):
        sl = slice(c * ff_chunk, (c + 1) * ff_chunk)
        gate = jnp.dot(hn, wg_ref[:, sl], preferred_element_type=_F32)
        up = jnp.dot(hn, wu_ref[:, sl], preferred_element_type=_F32)
        act = (gate / (1.0 + jnp.exp(-gate)) * up).astype(_BF16)
        acc = acc + jnp.dot(act, wd_ref[sl, :], preferred_element_type=_F32)
    return x + 0.5 * acc


def _ffn_mix_in_kernel(x_ref, g1_ref, wg_ref, wu_ref, wd_ref, g_ref, waf_ref, wq_ref, wk_ref, wvt_ref,
                       chi_ref, clo_ref, x_out_ref, a_ref, z_ref, q_ref, k_ref, vt_ref, *, ff_chunk):
    x = _swiglu_half_step(x_ref[0], g1_ref, wg_ref, wu_ref, wd_ref, ff_chunk)
    x_out_ref[0] = x
    hn = _rms_scale(x, g_ref[...]).astype(_BF16)
    af = jnp.dot(hn, waf_ref[...], preferred_element_type=_F32)
    a_ref[0] = af[:, :POOL_WIDTH]
    f_hi, f_lo = _split_hi_lo(af[:, POOL_WIDTH:])
    z_ref[0] = _dot3(f_hi, f_lo, chi_ref[...], clo_ref[...])
    q = jnp.dot(hn, wq_ref[...], preferred_element_type=_F32)
    q_ref[0] = (q * (HEAD_DIM ** -0.5 * LOG2_E)).astype(_BF16)
    k_ref[0] = jnp.dot(hn, wk_ref[...], preferred_element_type=_F32).astype(_BF16)
    vt = lax.dot_general(wvt_ref[...], hn, (((1,), (1,)), ((), ())), preferred_element_type=_F32)
    vt_ref[0] = vt.astype(_BF16)


def _ffn_mix_in(x, g1, wg, wu, wd, g, w_in, w_vt, layer, chan_hi, chan_lo, *, tm=512, ff_chunk=256):
    b, s, d = x.shape
    d_ff = wg.shape[2]
    const = lambda bi, i: (0, 0)
    row = lambda bi, i: (bi, i, 0)
    full = lambda arr: pl.BlockSpec(arr.shape, const)
    once = pl.Buffered(1)
    resident = lambda shape: pl.BlockSpec((None,) + shape, lambda bi, i: (layer, 0, 0), pipeline_mode=once)
    cols = lambda j: pl.BlockSpec((None, d, ATTN_WIDTH), lambda bi, i: (layer, 0, j), pipeline_mode=once)
    return pl.pallas_call(
        functools.partial(_ffn_mix_in_kernel, ff_chunk=ff_chunk),
        grid=(b, s // tm),
        in_specs=[pl.BlockSpec((1, tm, d), row), pl.BlockSpec((1, d), const),
                  resident((d, d_ff)), resident((d, d_ff)), resident((d_ff, d)),
                  pl.BlockSpec((1, d), const), cols(0), cols(1), cols(2), resident(w_vt.shape[1:]),
                  full(chan_hi), full(chan_lo)],
        out_specs=[pl.BlockSpec((1, tm, d), row),
                   pl.BlockSpec((1, tm, POOL_WIDTH), row),
                   pl.BlockSpec((1, tm, 2 * FOURIER_WIDTH), row),
                   pl.BlockSpec((1, tm, ATTN_WIDTH), row),
                   pl.BlockSpec((1, tm, ATTN_WIDTH), row),
                   pl.BlockSpec((1, ATTN_WIDTH, tm), lambda bi, i: (bi, 0, i))],
        out_shape=[jax.ShapeDtypeStruct((b, s, d), _F32),
                   jax.ShapeDtypeStruct((b, s, POOL_WIDTH), _F32),
                   jax.ShapeDtypeStruct((b, s, 2 * FOURIER_WIDTH), _F32),
                   jax.ShapeDtypeStruct((b, s, ATTN_WIDTH), _BF16),
                   jax.ShapeDtypeStruct((b, s, ATTN_WIDTH), _BF16),
                   jax.ShapeDtypeStruct((b, ATTN_WIDTH, s), _BF16)],
        compiler_params=_params("parallel", "parallel"),
        name="ffn_mix_in",
    )(x, g1.reshape(1, d), wg, wu, wd, g.reshape(1, d), w_in, w_in, w_in, w_vt, chan_hi, chan_lo)


def _dft_tables(s):
    h, l = s // DFT_L, DFT_L
    two_pi = 2.0 * np.pi
    c = np.arange(GROUP_DIM)
    ang = two_pi * np.outer(c, c) / GROUP_DIM
    eye = np.eye(FOURIER_WIDTH // GROUP_DIM)
    chan = np.concatenate([np.kron(eye, np.cos(ang)), np.kron(eye, -np.sin(ang))], axis=1)
    u = np.arange(h)
    ang1 = two_pi * np.outer(u, u) / h
    stage1 = np.concatenate([np.cos(ang1), -np.sin(ang1)], axis=0)
    t = np.arange(l)
    angt = two_pi * np.outer(u, t) / s
    tw_cos, tw_sin = np.cos(angt), np.sin(angt)
    ang2 = two_pi * np.outer(t, t) / l
    scale = 1.0 / math.sqrt(s * GROUP_DIM)
    stage2 = np.concatenate([np.cos(ang2), np.sin(ang2)], axis=1) * scale
    as_f32 = lambda a: jnp.asarray(a, _F32)
    return (_split_hi_lo(as_f32(chan)), _split_hi_lo(as_f32(stage1)), _split_hi_lo(as_f32(stage2)),
            jnp.repeat(as_f32(tw_cos), LANES, axis=1), jnp.repeat(as_f32(tw_sin), LANES, axis=1))


def _dft_stage1_kernel(z_ref, mhi_ref, mlo_ref, twc_ref, tws_ref, o_ref, *, slabs):
    h = z_ref.shape[1]
    w = FOURIER_WIDTH
    z = jnp.concatenate([z_ref[0, :, j, :] for j in range(slabs)], axis=1)
    z_hi, z_lo = _split_hi_lo(z)
    p = _dot3(mhi_ref[...], mlo_ref[...], z_hi, z_lo)
    for j in range(slabs):
        re, im = slice(2 * j * w, (2 * j + 1) * w), slice((2 * j + 1) * w, (2 * j + 2) * w)
        ar = p[:h, re] - p[h:, im]
        ai = p[:h, im] + p[h:, re]
        tw = slice(j * LANES, (j + 1) * LANES)
        tc = jnp.concatenate([twc_ref[:, tw]] * (w // LANES), axis=1)
        ts = jnp.concatenate([tws_ref[:, tw]] * (w // LANES), axis=1)
        o_ref[0, :, j, :] = jnp.concatenate([ar * tc + ai * ts, ai * tc - ar * ts], axis=1)


def _dft_stage1(z, m_hi, m_lo, tw_cos, tw_sin, *, slabs=16):
    b, s, zw = z.shape
    h = s // DFT_L
    z4 = z.reshape(b, h, DFT_L, zw)
    const = lambda bi, j: (0, 0)
    slab_block = pl.BlockSpec((1, h, slabs, zw), lambda bi, j: (bi, 0, j, 0))
    return pl.pallas_call(
        functools.partial(_dft_stage1_kernel, slabs=slabs),
        grid=(b, DFT_L // slabs),
        in_specs=[slab_block,
                  pl.BlockSpec(m_hi.shape, const), pl.BlockSpec(m_lo.shape, const),
                  pl.BlockSpec((h, slabs * LANES), lambda bi, j: (0, j)),
                  pl.BlockSpec((h, slabs * LANES), lambda bi, j: (0, j))],
        out_specs=slab_block,
        out_shape=jax.ShapeDtypeStruct(z4.shape, _F32),
        compiler_params=_params("parallel", "parallel"),
        name="dft_stage1",
    )(z4, m_hi, m_lo, tw_cos, tw_sin)


def _dft_stage2_kernel(a_ref, mhi_ref, mlo_ref, fw_ref, o_ref, *, rows):
    w = FOURIER_WIDTH
    for r in range(rows):
        blk = a_ref[0, r]
        stacked = jnp.concatenate([blk[:, :w], blk[:, w:]], axis=0)
        s_hi, s_lo = _split_hi_lo(stacked)
        y = _dot3(mhi_ref[...], mlo_ref[...], s_hi, s_lo)
        o_ref[0, :, r, :] = jnp.dot(y.astype(_BF16), fw_ref[...], preferred_element_type=_F32)


def _dft_stage2(a4, m_hi, m_lo, fourier_w, layer, *, rows=8):
    b, h, l, zw = a4.shape
    w = FOURIER_WIDTH
    const = lambda bi, j: (0, 0)
    out = pl.pallas_call(
        functools.partial(_dft_stage2_kernel, rows=rows),
        grid=(b, h // rows),
        in_specs=[pl.BlockSpec((1, rows, l, zw), lambda bi, j: (bi, j, 0, 0)),
                  pl.BlockSpec(m_hi.shape, const), pl.BlockSpec(m_lo.shape, const),
                  pl.BlockSpec((None,) + fourier_w.shape[1:], lambda bi, j: (layer, 0, 0))],
        out_specs=pl.BlockSpec((1, l, rows, w), lambda bi, j: (bi, 0, j, 0)),
        out_shape=jax.ShapeDtypeStruct((b, l, h, w), _F32),
        compiler_params=_params("parallel", "parallel"),
        name="dft_stage2",
    )(a4, m_hi, m_lo, fourier_w)
    return out.reshape(b, l * h, w)


def _bias_feature_tables(tile):
    pos = np.arange(tile)
    hi, lo = (pos // LANES) * LANES, pos % LANES
    qf = np.zeros((tile, LANES), np.float32)
    kf = np.zeros((tile, LANES), np.float32)
    rest = LOG2_E
    for p in range(3):
        part = float(np.float32(rest).astype(jnp.bfloat16).astype(np.float32))
        rest -= part
        qf[:, 2 * p], qf[:, 2 * p + 1] = hi, lo
        kf[:, 2 * p], kf[:, 2 * p + 1] = part, part
        qf[:, 6 + 2 * p], qf[:, 7 + 2 * p] = -part, -part
        kf[:, 6 + 2 * p], kf[:, 7 + 2 * p] = hi, lo
    return jnp.asarray(qf, _BF16), jnp.asarray(np.stack([-kf, 0.0 * kf, kf]), _BF16)


def _attn_kernel(lam_init_ref, lam_ref, gain_ref, qf_ref, kf_ref, q_ref, k_ref, vt_ref, o_ref,
                 w_ref, acc_ref, p_ref, knorm_ref, *, slope, reach, q_tiles, rolled):
    t = q_ref.shape[1] // q_tiles
    n = k_ref.shape[1] // t
    step = pl.program_id(1)
    c = slope * LOG2_E
    nt = (((1,), (1,)), ((), ()))
    half_rows = (lax.broadcasted_iota(jnp.int32, (8, LANES), 1) // HEAD_DIM
                 == lax.broadcasted_iota(jnp.int32, (8, LANES), 0)).astype(_F32)

    def half_norms(x):
        sq = jnp.square(x.astype(_F32))
        return jnp.sqrt(lax.dot_general(half_rows, sq, nt, preferred_element_type=_F32)[0:2])

    @pl.when(step == 0)
    def _():
        knorm_ref[0] = jnp.max(half_norms(k_ref[0]))

    def explicit_bias(tile_delta):
        dist = jnp.abs(lax.broadcasted_iota(jnp.int32, (t, t), 0) - lax.broadcasted_iota(jnp.int32, (t, t), 1)
                       + tile_delta * t)
        bias = dist.astype(_F32) * (-c)
        return jnp.concatenate([bias, bias], axis=1)

    def pv(tile_idx, p_buf):
        k0 = pl.multiple_of(tile_idx * t, t)
        return jnp.dot(vt_ref[0, :, pl.ds(k0, t)], p_buf[...], preferred_element_type=_F32)

    def query_tile(u):
        qi = step * q_tiles + u
        rows = slice(u * t, (u + 1) * t)
        w_u, acc_u, p_a, p_b = w_ref.at[u], acc_ref.at[u], p_ref.at[2 * u], p_ref.at[2 * u + 1]

        qh = q_ref[0, rows, :]
        lane = lax.broadcasted_iota(jnp.int32, qh.shape, 1)
        zero = jnp.zeros_like(qh)
        halves = jnp.concatenate([jnp.where(lane < HEAD_DIM, qh, zero), jnp.where(lane >= HEAD_DIM, qh, zero)],
                                 axis=0)
        qf = (qf_ref[...].astype(_F32) * slope).astype(_BF16)
        w_u[...] = jnp.concatenate([halves, jnp.concatenate([qf, qf], axis=0)], axis=1)
        qn = half_norms(qh)
        frame = jnp.concatenate([qn[0:1], qn[1:2]], axis=1) * (knorm_ref[0] * FRAME_SLACK) + 0.5

        def qk(tile_idx, side):
            k0 = pl.multiple_of(tile_idx * t, t)
            lhs = jnp.concatenate([k_ref[0, pl.ds(k0, t), :], kf_ref[side]], axis=1)
            return lax.dot_general(lhs, w_u[...], nt, preferred_element_type=_F32)

        def finish(o_all):
            lam_init = lam_init_ref[0]
            lp = lam_ref[...]
            lam = (jnp.exp(jnp.sum(lp[0:1] * lp[1:2], axis=-1, keepdims=True))
                   - jnp.exp(jnp.sum(lp[2:3] * lp[3:4], axis=-1, keepdims=True)) + lam_init)
            o = (o_all[:, :t] - lam * o_all[:, t:]).T
            o = _rms_scale(o, gain_ref[...]) * (1.0 - lam_init)
            o_ref[0, rows, :] = o.astype(o_ref.dtype)

        if 2 * reach + 1 >= n:
            others = [j + (qi <= j).astype(jnp.int32) for j in range(n - 1)]
            visits = [(qi, 1, None)] + [(i, jnp.where(i < qi, 0, 2), jnp.abs(i - qi).astype(_F32) * (-c * t))
                                        for i in others]
        else:
            visits = [(qi, 1, None)]
            for d in [sign * dist for dist in range(1, reach + 1) for sign in (-1, 1)]:
                i = qi + d
                inside = jnp.logical_and(i >= 0, i < n)
                visits.append((jnp.clip(i, 0, n - 1), 0 if d < 0 else 2,
                               jnp.where(inside, -c * t * abs(d), -NO_TILE)))

        acc_u[...] = jnp.zeros(acc_u.shape, _F32)
        bufs = (p_a, p_b)
        if rolled:
            assert 2 * reach + 1 >= n and n % 2 == 0
            other = lambda j: j + (qi <= j).astype(jnp.int32)

            def off_diagonal(j, l_run, cur, prev_tile, prev):
                i = other(j)
                off = jnp.abs(i - qi).astype(_F32) * (-c * t)
                p = jnp.exp2(qk(i, jnp.where(i < qi, 0, 2)) - (frame - off))
                cur[...] = p.astype(_BF16)
                acc_u[...] += pv(prev_tile, prev)
                return l_run + jnp.sum(p, axis=0, keepdims=True)

            p = jnp.exp2(qk(qi, 1) + explicit_bias(0) - frame)
            p_a[...] = p.astype(_BF16)
            l = off_diagonal(0, jnp.sum(p, axis=0, keepdims=True), p_b, qi, p_a)

            def pair(m, l_run):
                l_run = off_diagonal(2 * m - 1, l_run, p_a, other(2 * m - 2), p_b)
                return off_diagonal(2 * m, l_run, p_b, other(2 * m - 1), p_a)

            l = lax.fori_loop(1, n // 2, pair, l)
            o_all = (acc_u[...] + pv(other(n - 2), p_b)) / l
        else:
            l = jnp.zeros((1, 2 * t), _F32)
            for v, (i, side, off) in enumerate(visits):
                if off is None:
                    p = jnp.exp2(qk(i, side) + explicit_bias(0) - frame)
                else:
                    p = jnp.exp2(qk(i, side) - (frame - off))
                l = l + jnp.sum(p, axis=0, keepdims=True)
                bufs[v % 2][...] = p.astype(_BF16)
                if v > 0:
                    acc_u[...] += pv(visits[v - 1][0], bufs[(v - 1) % 2])
            last = len(visits) - 1
            o_all = (acc_u[...] + pv(visits[last][0], bufs[last % 2])) / l
        finish(o_all)
        unusable = jnp.logical_or(jnp.logical_not(jnp.sum(o_all * 0.0) == 0.0), jnp.min(l) < MIN_DENOMINATOR)

        def recompute():
            acc_u[...] = jnp.zeros(acc_u.shape, _F32)

            def safe_body(i, carry):
                m, l_run = carry
                sc = qk(i, 1) + explicit_bias(i - qi)
                m_new = jnp.maximum(m, jnp.max(sc, axis=0, keepdims=True))
                alpha = jnp.exp2(m - m_new)
                p = jnp.exp2(sc - m_new)
                p_a[...] = p.astype(_BF16)
                acc_u[...] = alpha * acc_u[...] + pv(i, p_a)
                return m_new, alpha * l_run + jnp.sum(p, axis=0, keepdims=True)

            init = (jnp.full((1, 2 * t), -jnp.inf, _F32), jnp.zeros((1, 2 * t), _F32))
            _, l_safe = lax.fori_loop(0, n, safe_body, init)
            finish(acc_u[...] / l_safe)

        return unusable, recompute

    for unusable, recompute in [query_tile(u) for u in range(q_tiles)]:
        pl.when(unusable)(recompute)


def _diff_attention_head(q, k, vt, head, slope, lam_init, lam_params, head_gain, *, tile=512):
    b, s, _ = q.shape
    n = s // tile
    qf, kf = _bias_feature_tables(tile)
    reach = int(math.floor((FLUSH_EXP / (slope * LOG2_E) - 1) / tile)) + 1
    rolled = head == 2
    q_tiles = 1 if rolled else 2 if 2 * reach + 1 >= n else 4
    assert n >= 2 and n % q_tiles == 0
    smem = pl.BlockSpec(memory_space=pltpu.SMEM)
    const = lambda shape: pl.BlockSpec(shape, lambda bi, qi: (0,) * len(shape))
    return pl.pallas_call(
        functools.partial(_attn_kernel, slope=slope, reach=reach, q_tiles=q_tiles, rolled=rolled),
        grid=(b, n // q_tiles),
        in_specs=[smem, const(lam_params.shape),
                  pl.BlockSpec((1, V_DIM), lambda bi, qi: (0, head)),
                  const(qf.shape), const(kf.shape),
                  pl.BlockSpec((1, q_tiles * tile, V_DIM), lambda bi, qi: (bi, qi, head)),
                  pl.BlockSpec((1, s, V_DIM), lambda bi, qi: (bi, 0, head)),
                  pl.BlockSpec((1, V_DIM, s), lambda bi, qi: (bi, head, 0))],
        out_specs=pl.BlockSpec((1, q_tiles * tile, V_DIM), lambda bi, qi: (bi, qi, 0)),
        out_shape=jax.ShapeDtypeStruct((b, s, V_DIM), _BF16),
        scratch_shapes=[pltpu.VMEM((q_tiles, 2 * tile, 2 * LANES), _BF16),
                        pltpu.VMEM((q_tiles, V_DIM, 2 * tile), _F32),
                        pltpu.VMEM((2 * q_tiles, tile, 2 * tile), _BF16),
                        pltpu.SMEM((1,), _F32)],
        compiler_params=_params("arbitrary", "arbitrary"),
        name=f"diff_attn_h{head}",
    )(lam_init, lam_params, head_gain.reshape(1, ATTN_WIDTH), qf, kf, q, k, vt)


def _mix_out_ffn_kernel(x_ref, a_ref, a_prev_ref, a_next_ref, pw_ref, ps_ref, yf_ref, *rest, ff_chunk, final):
    head_refs, rest = rest[:ATTN_HEADS], rest[ATTN_HEADS:]
    if final:
        wo_ref, g_ref, wg_ref, wu_ref, wd_ref, gf_ref, o_ref, buf_ref = rest
    else:
        wo_ref, g_ref, wg_ref, wu_ref, wd_ref, o_ref, buf_ref = rest
    i = pl.program_id(1)
    tm = a_ref.shape[1]
    s_len = tm * pl.num_programs(1)
    halo = POOL_HALO
    buf_ref[0:halo] = jnp.where(i > 0, a_prev_ref[0], 0.0)
    buf_ref[halo:halo + tm] = a_ref[0]
    buf_ref[halo + tm:] = jnp.where(i < pl.num_programs(1) - 1, a_next_ref[0], 0.0)
    lane = lax.broadcasted_iota(jnp.int32, (1, LANES), 1)
    pick = lambda v0, v1: jnp.where(lane >= GROUP_DIM, v1, v0)

    def pooled(lanes, windows):
        shifted = lambda d: buf_ref[halo + d:halo + d + tm, lanes]
        a = shifted(0)
        sums, total, lo, hi = [], a, 0, 0
        for w in windows:
            left, right = w // 2, w - 1 - w // 2
            for d in list(range(-left, lo)) + list(range(hi + 1, right + 1)):
                total = total + shifted(d)
            lo, hi = -left, right
            sums.append(total)
        win = pick(*sums)
        mixed = win * pick(*[1.0 / w for w in windows]) - a

        def edge(rows):
            t = i * tm + rows.start + lax.broadcasted_iota(jnp.int32, (halo, LANES), 0)
            left, right = pick(*[w // 2 for w in windows]), pick(*[w - 1 - w // 2 for w in windows])
            cnt = jnp.minimum(t + right + 1, s_len) - jnp.maximum(t - left, 0)
            return win[rows] / cnt.astype(_F32) - a[rows]

        return jnp.concatenate([edge(slice(0, halo)), mixed[halo:tm - halo], edge(slice(tm - halo, tm))], axis=0)

    mixed = jnp.concatenate([pooled(slice(0, LANES), POOL_WINDOWS[:2]),
                             pooled(slice(LANES, 2 * LANES), POOL_WINDOWS[2:])], axis=1)
    y_pool = jnp.dot(mixed.astype(_BF16), pw_ref[...], preferred_element_type=_F32) * ps_ref[...]

    rest = jnp.concatenate([yf_ref[0].astype(_BF16)] + [r[0] for r in head_refs], axis=1)
    y = jnp.dot(rest, wo_ref[POOL_WIDTH:, :], preferred_element_type=_F32)
    y = y + jnp.dot(y_pool.astype(_BF16), wo_ref[:POOL_WIDTH, :], preferred_element_type=_F32)
    x = _swiglu_half_step(x_ref[0] + y, g_ref, wg_ref, wu_ref, wd_ref, ff_chunk)
    o_ref[0] = _rms_scale(x, gf_ref[...]) if final else x


def _mix_out_ffn(x, a, pool_bd, pool_scale, y_four, y_heads, w_out, g, wg, wu, wd, layer, final_g=None,
                 *, tm=512, ff_chunk=256):
    b, s, d = x.shape
    d_ff = wg.shape[2]
    hb = tm // POOL_HALO
    n_halo = s // POOL_HALO
    row = lambda bi, i: (bi, i, 0)
    const = lambda bi, i: (0, 0)
    resident = lambda shape: pl.BlockSpec((None,) + shape, lambda bi, i: (layer, 0, 0),
                                          pipeline_mode=pl.Buffered(1))
    in_specs = ([pl.BlockSpec((1, tm, d), row),
                 pl.BlockSpec((1, tm, POOL_WIDTH), row),
                 pl.BlockSpec((1, POOL_HALO, POOL_WIDTH), lambda bi, i: (bi, jnp.maximum(i * hb - 1, 0), 0)),
                 pl.BlockSpec((1, POOL_HALO, POOL_WIDTH),
                              lambda bi, i: (bi, jnp.minimum((i + 1) * hb, n_halo - 1), 0)),
                 pl.BlockSpec(pool_bd.shape, const), pl.BlockSpec((1, POOL_WIDTH), const),
                 pl.BlockSpec((1, tm, FOURIER_WIDTH), row)]
                + [pl.BlockSpec((1, tm, V_DIM), row)] * ATTN_HEADS
                + [resident((d, d)), pl.BlockSpec((1, d), const),
                   resident((d, d_ff)), resident((d, d_ff)), resident((d_ff, d))])
    args = [x, a, a, a, pool_bd, pool_scale.reshape(1, POOL_WIDTH), y_four, *y_heads, w_out,
            g.reshape(1, d), wg, wu, wd]
    if final_g is not None:
        in_specs.append(pl.BlockSpec((1, d), const))
        args.append(final_g.reshape(1, d))
    return pl.pallas_call(
        functools.partial(_mix_out_ffn_kernel, ff_chunk=ff_chunk, final=final_g is not None),
        grid=(b, s // tm),
        in_specs=in_specs,
        out_specs=pl.BlockSpec((1, tm, d), row),
        out_shape=jax.ShapeDtypeStruct((b, s, d), _F32),
        scratch_shapes=[pltpu.VMEM((tm + 2 * POOL_HALO, POOL_WIDTH), _F32)],
        compiler_params=_params("parallel", "parallel"),
        name="mix_out_ffn",
    )(*args)


def kernel(x, ffn1_norm, ffn1_w_gate, ffn1_w_up, ffn1_w_down, mix_norm, w_in, pool_w, pool_scale, fourier_w, lam_q1, lam_k1, lam_q2, lam_k2, attn_head_norm, w_out, ffn2_norm, ffn2_w_gate, ffn2_w_up, ffn2_w_down, final_norm):
    b, s, d = x.shape
    depth = w_in.shape[0]
    bf = lambda w: w.astype(_BF16)
    (chan_hi, chan_lo), (m1_hi, m1_lo), (m2_hi, m2_lo), tw_cos, tw_sin = _dft_tables(s)
    alibi = [2.0 ** (-8.0 * (i + 1) / ATTN_HEADS) for i in range(ATTN_HEADS)]
    assert all(math.frexp(v)[0] == 0.5 for v in alibi), "position features rely on power-of-two slopes"
    assert POOL_WIDTH + FOURIER_WIDTH == ATTN_WIDTH
    assert all(w & (w - 1) == 0 and w <= 2 * POOL_HALO for w in POOL_WINDOWS) and POOL_WIDTH == 2 * LANES
    ffn1 = (bf(ffn1_w_gate), bf(ffn1_w_up), bf(ffn1_w_down))
    ffn2 = (bf(ffn2_w_gate), bf(ffn2_w_up), bf(ffn2_w_down))
    w_in_b, w_out_b, fourier_b = bf(w_in), bf(w_out), bf(fourier_w)
    w_vt_b = bf(jnp.swapaxes(w_in[:, :, 3 * ATTN_WIDTH:], 1, 2))

    for l in range(depth):
        x, a, z, q, k, vt = _ffn_mix_in(x, ffn1_norm[l], *ffn1, mix_norm[l], w_in_b, w_vt_b, l, chan_hi, chan_lo)
        y_four = _dft_stage2(_dft_stage1(z, m1_hi, m1_lo, tw_cos, tw_sin), m2_hi, m2_lo, fourier_b, l)
        lam_init = jnp.full((1,), 0.8 - 0.6 * math.exp(-0.3 * l), _F32)
        lam_params = jnp.stack([lam_q1[l], lam_k1[l], lam_q2[l], lam_k2[l]]).astype(_F32)
        y_heads = [_diff_attention_head(q, k, vt, h, alibi[h], lam_init, lam_params, attn_head_norm[l])
                   for h in range(ATTN_HEADS)]
        pool_bd = jax.scipy.linalg.block_diag(*[pool_w[l, g] for g in range(len(POOL_WINDOWS))])
        x = _mix_out_ffn(x, a, bf(pool_bd), pool_scale[l], y_four, y_heads, w_out_b, ffn2_norm[l], *ffn2, l,
                         final_norm if l == depth - 1 else None)
    return x
```

```python
import functools
import math

import numpy as np
import jax
import jax.numpy as jnp
from jax import lax
from jax.experimental import pallas as pl
from jax.experimental.pallas import tpu as pltpu

_F32 = jnp.float32
_BF16 = jnp.bfloat16

NORM_EPS = 1e-6
POOL_WIDTH = 256
FOURIER_WIDTH = 256
ATTN_WIDTH = 512
POOL_WINDOWS = (2, 4, 8, 16)
GROUP_DIM = 64
ATTN_HEADS = 4
HEAD_DIM = 64
V_DIM = 2 * HEAD_DIM
POOL_HALO = 8
DFT_L = 128
LANES = 128
LOG2_E = math.log2(math.e)
FLUSH_EXP = 127.0
FRAME_SLACK = 1.0 + 2.0 ** -5
NO_TILE = 1e30
MIN_DENOMINATOR = 2.0 ** -90

_VMEM_LIMIT = 56 * 1024 * 1024


def _params(*semantics, flags=None):
    return pltpu.CompilerParams(dimension_semantics=semantics, vmem_limit_bytes=_VMEM_LIMIT, flags=flags)


def _rms_scale(x, g):
    ms = jnp.mean(x * x, axis=-1, keepdims=True)
    return x * lax.rsqrt(ms + NORM_EPS) * g


def _split_hi_lo(x):
    hi = x.astype(_BF16)
    lo = (x - hi.astype(_F32)).astype(_BF16)
    return hi, lo


def _dot3(a_hi, a_lo, b_hi, b_lo):
    d = functools.partial(jnp.dot, preferred_element_type=_F32)
    return d(a_hi, b_hi) + (d(a_lo, b_hi) + d(a_hi, b_lo))


def _swiglu_half_step(x, g_ref, wg_ref, wu_ref, wd_ref, ff_chunk):
    hn = _rms_scale(x, g_ref[...]).astype(_BF16)
    acc = jnp.zeros(x.shape, _F32)
    for c in range(wg_ref.shape[1] // ff_chunk):
        sl = slice(c * ff_chunk, (c + 1) * ff_chunk)
        gate = jnp.dot(hn, wg_ref[:, sl], preferred_element_type=_F32)
        up = jnp.dot(hn, wu_ref[:, sl], preferred_element_type=_F32)
        act = (gate / (1.0 + jnp.exp(-gate)) * up).astype(_BF16)
        acc = acc + jnp.dot(act, wd_ref[sl, :], preferred_element_type=_F32)
    return x + 0.5 * acc


def _ffn_mix_in_kernel(x_ref, g1_ref, wg_ref, wu_ref, wd_ref, g_ref, waf_ref, wq_ref, wk_ref, wvt_ref,
                       chi_ref, clo_ref, x_out_ref, a_ref, z_ref, q_ref, k_ref, vt_ref, *, ff_chunk):
    x = _swiglu_half_step(x_ref[0], g1_ref, wg_ref, wu_ref, wd_ref, ff_chunk)
    x_out_ref[0] = x
    hn = _rms_scale(x, g_ref[...]).astype(_BF16)
    af = jnp.dot(hn, waf_ref[...], preferred_element_type=_F32)
    a_ref[0] = af[:, :POOL_WIDTH]
    f_hi, f_lo = _split_hi_lo(af[:, POOL_WIDTH:])
    z_ref[0] = _dot3(f_hi, f_lo, chi_ref[...], clo_ref[...])
    q = jnp.dot(hn, wq_ref[...], preferred_element_type=_F32)
    q_ref[0] = (q * (HEAD_DIM ** -0.5 * LOG2_E)).astype(_BF16)
    k_ref[0] = jnp.dot(hn, wk_ref[...], preferred_element_type=_F32).astype(_BF16)
    vt = lax.dot_general(wvt_ref[...], hn, (((1,), (1,)), ((), ())), preferred_element_type=_F32)
    vt_ref[0] = vt.astype(_BF16)


def _ffn_mix_in(x, g1, wg, wu, wd, g, w_in, w_vt, layer, chan_hi, chan_lo, *, tm=512, ff_chunk=256):
    b, s, d = x.shape
    d_ff = wg.shape[2]
    const = lambda bi, i: (0, 0)
    row = lambda bi, i: (bi, i, 0)
    full = lambda arr: pl.BlockSpec(arr.shape, const)
    once = pl.Buffered(1)
    resident = lambda shape: pl.BlockSpec((None,) + shape, lambda bi, i: (layer, 0, 0), pipeline_mode=once)
    cols = lambda j: pl.BlockSpec((None, d, ATTN_WIDTH), lambda bi, i: (layer, 0, j), pipeline_mode=once)
    return pl.pallas_call(
        functools.partial(_ffn_mix_in_kernel, ff_chunk=ff_chunk),
        grid=(b, s // tm),
        in_specs=[pl.BlockSpec((1, tm, d), row), pl.BlockSpec((1, d), const),
                  resident((d, d_ff)), resident((d, d_ff)), resident((d_ff, d)),
                  pl.BlockSpec((1, d), const), cols(0), cols(1), cols(2), resident(w_vt.shape[1:]),
                  full(chan_hi), full(chan_lo)],
        out_specs=[pl.BlockSpec((1, tm, d), row),
                   pl.BlockSpec((1, tm, POOL_WIDTH), row),
                   pl.BlockSpec((1, tm, 2 * FOURIER_WIDTH), row),
                   pl.BlockSpec((1, tm, ATTN_WIDTH), row),
                   pl.BlockSpec((1, tm, ATTN_WIDTH), row),
                   pl.BlockSpec((1, ATTN_WIDTH, tm), lambda bi, i: (bi, 0, i))],
        out_shape=[jax.ShapeDtypeStruct((b, s, d), _F32),
                   jax.ShapeDtypeStruct((b, s, POOL_WIDTH), _F32),
                   jax.ShapeDtypeStruct((b, s, 2 * FOURIER_WIDTH), _F32),
                   jax.ShapeDtypeStruct((b, s, ATTN_WIDTH), _BF16),
                   jax.ShapeDtypeStruct((b, s, ATTN_WIDTH), _BF16),
                   jax.ShapeDtypeStruct((b, ATTN_WIDTH, s), _BF16)],
        compiler_params=_params("parallel", "parallel"),
        name="ffn_mix_in",
    )(x, g1.reshape(1, d), wg, wu, wd, g.reshape(1, d), w_in, w_in, w_in, w_vt, chan_hi, chan_lo)


def _dft_tables(s):
    h, l = s // DFT_L, DFT_L
    two_pi = 2.0 * np.pi
    c = np.arange(GROUP_DIM)
    ang = two_pi * np.outer(c, c) / GROUP_DIM
    eye = np.eye(FOURIER_WIDTH // GROUP_DIM)
    chan = np.concatenate([np.kron(eye, np.cos(ang)), np.kron(eye, -np.sin(ang))], axis=1)
    u = np.arange(h)
    ang1 = two_pi * np.outer(u, u) / h
    stage1 = np.concatenate([np.cos(ang1), -np.sin(ang1)], axis=0)
    t = np.arange(l)
    angt = two_pi * np.outer(u, t) / s
    tw_cos, tw_sin = np.cos(angt), np.sin(angt)
    ang2 = two_pi * np.outer(t, t) / l
    scale = 1.0 / math.sqrt(s * GROUP_DIM)
    stage2 = np.concatenate([np.cos(ang2), np.sin(ang2)], axis=1) * scale
    as_f32 = lambda a: jnp.asarray(a, _F32)
    return (_split_hi_lo(as_f32(chan)), _split_hi_lo(as_f32(stage1)), _split_hi_lo(as_f32(stage2)),
            jnp.repeat(as_f32(tw_cos), LANES, axis=1), jnp.repeat(as_f32(tw_sin), LANES, axis=1))


def _dft_stage1_kernel(z_ref, mhi_ref, mlo_ref, twc_ref, tws_ref, o_ref, *, slabs):
    h = z_ref.shape[1]
    w = FOURIER_WIDTH
    z = jnp.concatenate([z_ref[0, :, j, :] for j in range(slabs)], axis=1)
    z_hi, z_lo = _split_hi_lo(z)
    p = _dot3(mhi_ref[...], mlo_ref[...], z_hi, z_lo)
    for j in range(slabs):
        re, im = slice(2 * j * w, (2 * j + 1) * w), slice((2 * j + 1) * w, (2 * j + 2) * w)
        ar = p[:h, re] - p[h:, im]
        ai = p[:h, im] + p[h:, re]
        tw = slice(j * LANES, (j + 1) * LANES)
        tc = jnp.concatenate([twc_ref[:, tw]] * (w // LANES), axis=1)
        ts = jnp.concatenate([tws_ref[:, tw]] * (w // LANES), axis=1)
        o_ref[0, :, j, :] = jnp.concatenate([ar * tc + ai * ts, ai * tc - ar * ts], axis=1)


def _dft_stage1(z, m_hi, m_lo, tw_cos, tw_sin, *, slabs=16):
    b, s, zw = z.shape
    h = s // DFT_L
    z4 = z.reshape(b, h, DFT_L, zw)
    const = lambda bi, j: (0, 0)
    slab_block = pl.BlockSpec((1, h, slabs, zw), lambda bi, j: (bi, 0, j, 0))
    return pl.pallas_call(
        functools.partial(_dft_stage1_kernel, slabs=slabs),
        grid=(b, DFT_L // slabs),
        in_specs=[slab_block,
                  pl.BlockSpec(m_hi.shape, const), pl.BlockSpec(m_lo.shape, const),
                  pl.BlockSpec((h, slabs * LANES), lambda bi, j: (0, j)),
                  pl.BlockSpec((h, slabs * LANES), lambda bi, j: (0, j))],
        out_specs=slab_block,
        out_shape=jax.ShapeDtypeStruct(z4.shape, _F32),
        compiler_params=_params("parallel", "parallel"),
        name="dft_stage1",
    )(z4, m_hi, m_lo, tw_cos, tw_sin)


def _dft_stage2_kernel(a_ref, mhi_ref, mlo_ref, fw_ref, o_ref, *, rows):
    w = FOURIER_WIDTH
    for r in range(rows):
        blk = a_ref[0, r]
        stacked = jnp.concatenate([blk[:, :w], blk[:, w:]], axis=0)
        s_hi, s_lo = _split_hi_lo(stacked)
        y = _dot3(mhi_ref[...], mlo_ref[...], s_hi, s_lo)
        o_ref[0, :, r, :] = jnp.dot(y.astype(_BF16), fw_ref[...], preferred_element_type=_F32)


def _dft_stage2(a4, m_hi, m_lo, fourier_w, layer, *, rows=8):
    b, h, l, zw = a4.shape
    w = FOURIER_WIDTH
    const = lambda bi, j: (0, 0)
    out = pl.pallas_call(
        functools.partial(_dft_stage2_kernel, rows=rows),
        grid=(b, h // rows),
        in_specs=[pl.BlockSpec((1, rows, l, zw), lambda bi, j: (bi, j, 0, 0)),
                  pl.BlockSpec(m_hi.shape, const), pl.BlockSpec(m_lo.shape, const),
                  pl.BlockSpec((None,) + fourier_w.shape[1:], lambda bi, j: (layer, 0, 0))],
        out_specs=pl.BlockSpec((1, l, rows, w), lambda bi, j: (bi, 0, j, 0)),
        out_shape=jax.ShapeDtypeStruct((b, l, h, w), _F32),
        compiler_params=_params("parallel", "parallel"),
        name="dft_stage2",
    )(a4, m_hi, m_lo, fourier_w)
    return out.reshape(b, l * h, w)


def _bias_feature_tables(tile):
    pos = np.arange(tile)
    hi, lo = (pos // LANES) * LANES, pos % LANES
    qf = np.zeros((tile, LANES), np.float32)
    kf = np.zeros((tile, LANES), np.float32)
    rest = LOG2_E
    for p in range(3):
        part = float(np.float32(rest).astype(jnp.bfloat16).astype(np.float32))
        rest -= part
        qf[:, 2 * p], qf[:, 2 * p + 1] = hi, lo
        kf[:, 2 * p], kf[:, 2 * p + 1] = part, part
        qf[:, 6 + 2 * p], qf[:, 7 + 2 * p] = -part, -part
        kf[:, 6 + 2 * p], kf[:, 7 + 2 * p] = hi, lo
    return jnp.asarray(qf, _BF16), jnp.asarray(np.stack([-kf, 0.0 * kf, kf]), _BF16)


def _attn_kernel(lam_init_ref, lam_ref, gain_ref, qft_ref, kf_ref, q_ref, k_ref, vt_ref, o_ref,
                 w_ref, acc_ref, p_ref, knorm_ref, *, slope, reach, q_tiles):
    t = q_ref.shape[1] // q_tiles
    n = k_ref.shape[1] // t
    step = pl.program_id(1)
    c = slope * LOG2_E
    nt = (((1,), (1,)), ((), ()))
    half_rows = (lax.broadcasted_iota(jnp.int32, (8, LANES), 1) // HEAD_DIM
                 == lax.broadcasted_iota(jnp.int32, (8, LANES), 0)).astype(_F32)

    def half_norms(x):
        sq = jnp.square(x.astype(_F32))
        return jnp.sqrt(lax.dot_general(half_rows, sq, nt, preferred_element_type=_F32)[0:2])

    @pl.when(step == 0)
    def _():
        knorm_ref[0] = jnp.max(half_norms(k_ref[0]))

    def explicit_bias(tile_delta):
        dist = jnp.abs(lax.broadcasted_iota(jnp.int32, (t, t), 0) - lax.broadcasted_iota(jnp.int32, (t, t), 1)
                       + tile_delta * t)
        bias = dist.astype(_F32) * (-c)
        return jnp.concatenate([bias, bias], axis=1)

    def pv(tile_idx, p_buf):
        k0 = pl.multiple_of(tile_idx * t, t)
        return jnp.dot(vt_ref[0, :, pl.ds(k0, t)], p_buf[...], preferred_element_type=_F32)

    def query_tile(u):
        qi = step * q_tiles + u
        rows = slice(u * t, (u + 1) * t)
        w_u, acc_u, p_a, p_b = w_ref.at[u], acc_ref.at[u], p_ref.at[2 * u], p_ref.at[2 * u + 1]

        qh = q_ref[0, rows, :]
        q_t = qh.astype(_F32).T
        row = lax.broadcasted_iota(jnp.int32, q_t.shape, 0)
        zero = jnp.zeros_like(q_t)
        halves = jnp.concatenate([jnp.where(row < HEAD_DIM, q_t, zero), jnp.where(row >= HEAD_DIM, q_t, zero)],
                                 axis=1)
        qf_t = qft_ref[...].astype(_F32) * slope
        w_u[...] = jnp.concatenate([halves, jnp.concatenate([qf_t, qf_t], axis=1)], axis=0).astype(_BF16)
        qn = half_norms(qh)
        frame = jnp.concatenate([qn[0:1], qn[1:2]], axis=1) * (knorm_ref[0] * FRAME_SLACK) + 0.5

        def qk(tile_idx, side):
            k0 = pl.multiple_of(tile_idx * t, t)
            lhs = jnp.concatenate([k_ref[0, pl.ds(k0, t), :], kf_ref[side]], axis=1)
            return jnp.dot(lhs, w_u[...], preferred_element_type=_F32)

        def finish(o_all):
            lam_init = lam_init_ref[0]
            lp = lam_ref[...]
            lam = (jnp.exp(jnp.sum(lp[0:1] * lp[1:2], axis=-1, keepdims=True))
                   - jnp.exp(jnp.sum(lp[2:3] * lp[3:4], axis=-1, keepdims=True)) + lam_init)
            o = (o_all[:, :t] - lam * o_all[:, t:]).T
            o = _rms_scale(o, gain_ref[...]) * (1.0 - lam_init)
            o_ref[0, rows, :] = o.astype(o_ref.dtype)

        if 2 * reach + 1 >= n:
            others = [j + (qi <= j).astype(jnp.int32) for j in range(n - 1)]
            visits = [(qi, 1, None)] + [(i, jnp.where(i < qi, 0, 2), jnp.abs(i - qi).astype(_F32) * (-c * t))
                                        for i in others]
        else:
            visits = [(qi, 1, None)]
            for d in [sign * dist for dist in range(1, reach + 1) for sign in (-1, 1)]:
                i = qi + d
                inside = jnp.logical_and(i >= 0, i < n)
                visits.append((jnp.clip(i, 0, n - 1), 0 if d < 0 else 2,
                               jnp.where(inside, -c * t * abs(d), -NO_TILE)))

        acc_u[...] = jnp.zeros(acc_u.shape, _F32)
        bufs = (p_a, p_b)
        l = jnp.zeros((1, 2 * t), _F32)
        for v, (i, side, off) in enumerate(visits):
            if off is None:
                p = jnp.exp2(qk(i, side) + explicit_bias(0) - frame)
            else:
                p = jnp.exp2(qk(i, side) - (frame - off))
            l = l + jnp.sum(p, axis=0, keepdims=True)
            bufs[v % 2][...] = p.astype(_BF16)
            if v > 0:
                acc_u[...] += pv(visits[v - 1][0], bufs[(v - 1) % 2])
        last = len(visits) - 1
        o_all = (acc_u[...] + pv(visits[last][0], bufs[last % 2])) / l
        finish(o_all)
        unusable = jnp.logical_or(jnp.logical_not(jnp.sum(o_all * 0.0) == 0.0), jnp.min(l) < MIN_DENOMINATOR)

        def recompute():
            acc_u[...] = jnp.zeros(acc_u.shape, _F32)

            def safe_body(i, carry):
                m, l_run = carry
                sc = qk(i, 1) + explicit_bias(i - qi)
                m_new = jnp.maximum(m, jnp.max(sc, axis=0, keepdims=True))
                alpha = jnp.exp2(m - m_new)
                p = jnp.exp2(sc - m_new)
                p_a[...] = p.astype(_BF16)
                acc_u[...] = alpha * acc_u[...] + pv(i, p_a)
                return m_new, alpha * l_run + jnp.sum(p, axis=0, keepdims=True)

            init = (jnp.full((1, 2 * t), -jnp.inf, _F32), jnp.zeros((1, 2 * t), _F32))
            _, l_safe = lax.fori_loop(0, n, safe_body, init)
            finish(acc_u[...] / l_safe)

        return unusable, recompute

    for unusable, recompute in [query_tile(u) for u in range(q_tiles)]:
        pl.when(unusable)(recompute)


def _diff_attention_head(q, k, vt, head, slope, lam_init, lam_params, head_gain, *, tile=512):
    b, s, _ = q.shape
    n = s // tile
    qf, kf = _bias_feature_tables(tile)
    qf_t = qf.T
    reach = int(math.floor((FLUSH_EXP / (slope * LOG2_E) - 1) / tile)) + 1
    q_tiles = 2 if 2 * reach + 1 >= n else 4
    assert n >= 2 and n % q_tiles == 0
    smem = pl.BlockSpec(memory_space=pltpu.SMEM)
    const = lambda shape: pl.BlockSpec(shape, lambda bi, qi: (0,) * len(shape))
    return pl.pallas_call(
        functools.partial(_attn_kernel, slope=slope, reach=reach, q_tiles=q_tiles),
        grid=(b, n // q_tiles),
        in_specs=[smem, const(lam_params.shape),
                  pl.BlockSpec((1, V_DIM), lambda bi, qi: (0, head)),
                  const(qf_t.shape), const(kf.shape),
                  pl.BlockSpec((1, q_tiles * tile, V_DIM), lambda bi, qi: (bi, qi, head)),
                  pl.BlockSpec((1, s, V_DIM), lambda bi, qi: (bi, 0, head)),
                  pl.BlockSpec((1, V_DIM, s), lambda bi, qi: (bi, head, 0))],
        out_specs=pl.BlockSpec((1, q_tiles * tile, V_DIM), lambda bi, qi: (bi, qi, 0)),
        out_shape=jax.ShapeDtypeStruct((b, s, V_DIM), _BF16),
        scratch_shapes=[pltpu.VMEM((q_tiles, 2 * LANES, 2 * tile), _BF16),
                        pltpu.VMEM((q_tiles, V_DIM, 2 * tile), _F32),
                        pltpu.VMEM((2 * q_tiles, tile, 2 * tile), _BF16),
                        pltpu.SMEM((1,), _F32)],
        compiler_params=_params("arbitrary", "arbitrary"),
        name=f"diff_attn_h{head}",
    )(lam_init, lam_params, head_gain.reshape(1, ATTN_WIDTH), qf_t, kf, q, k, vt)


def _mix_out_ffn_kernel(x_ref, a_ref, a_prev_ref, a_next_ref, pw_ref, ps_ref, yf_ref, *rest, ff_chunk, final):
    head_refs, rest = rest[:ATTN_HEADS], rest[ATTN_HEADS:]
    if final:
        wo_ref, g_ref, wg_ref, wu_ref, wd_ref, gf_ref, o_ref, buf_ref = rest
    else:
        wo_ref, g_ref, wg_ref, wu_ref, wd_ref, o_ref, buf_ref = rest
    i = pl.program_id(1)
    tm = a_ref.shape[1]
    s_len = tm * pl.num_programs(1)
    halo = POOL_HALO
    buf_ref[0:halo] = jnp.where(i > 0, a_prev_ref[0], 0.0)
    buf_ref[halo:halo + tm] = a_ref[0]
    buf_ref[halo + tm:] = jnp.where(i < pl.num_programs(1) - 1, a_next_ref[0], 0.0)
    lane = lax.broadcasted_iota(jnp.int32, (1, LANES), 1)
    pick = lambda v0, v1: jnp.where(lane >= GROUP_DIM, v1, v0)

    def pooled(lanes, windows):
        shifted = lambda d: buf_ref[halo + d:halo + d + tm, lanes]
        a = shifted(0)
        sums, total, lo, hi = [], a, 0, 0
        for w in windows:
            left, right = w // 2, w - 1 - w // 2
            for d in list(range(-left, lo)) + list(range(hi + 1, right + 1)):
                total = total + shifted(d)
            lo, hi = -left, right
            sums.append(total)
        win = pick(*sums)
        mixed = win * pick(*[1.0 / w for w in windows]) - a

        def edge(rows):
            t = i * tm + rows.start + lax.broadcasted_iota(jnp.int32, (halo, LANES), 0)
            left, right = pick(*[w // 2 for w in windows]), pick(*[w - 1 - w // 2 for w in windows])
            cnt = jnp.minimum(t + right + 1, s_len) - jnp.maximum(t - left, 0)
            return win[rows] / cnt.astype(_F32) - a[rows]

        return jnp.concatenate([edge(slice(0, halo)), mixed[halo:tm - halo], edge(slice(tm - halo, tm))], axis=0)

    mixed = jnp.concatenate([pooled(slice(0, LANES), POOL_WINDOWS[:2]),
                             pooled(slice(LANES, 2 * LANES), POOL_WINDOWS[2:])], axis=1)
    y_pool = jnp.dot(mixed.astype(_BF16), pw_ref[...], preferred_element_type=_F32) * ps_ref[...]

    rest = jnp.concatenate([yf_ref[0].astype(_BF16)] + [r[0] for r in head_refs], axis=1)
    y = jnp.dot(rest, wo_ref[POOL_WIDTH:, :], preferred_element_type=_F32)
    y = y + jnp.dot(y_pool.astype(_BF16), wo_ref[:POOL_WIDTH, :], preferred_element_type=_F32)
    x = _swiglu_half_step(x_ref[0] + y, g_ref, wg_ref, wu_ref, wd_ref, ff_chunk)
    o_ref[0] = _rms_scale(x, gf_ref[...]) if final else x


def _mix_out_ffn(x, a, pool_bd, pool_scale, y_four, y_heads, w_out, g, wg, wu, wd, layer, final_g=None,
                 *, tm=512, ff_chunk=256):
    b, s, d = x.shape
    d_ff = wg.shape[2]
    hb = tm // POOL_HALO
    n_halo = s // POOL_HALO
    row = lambda bi, i: (bi, i, 0)
    const = lambda bi, i: (0, 0)
    resident = lambda shape: pl.BlockSpec((None,) + shape, lambda bi, i: (layer, 0, 0),
                                          pipeline_mode=pl.Buffered(1))
    in_specs = ([pl.BlockSpec((1, tm, d), row),
                 pl.BlockSpec((1, tm, POOL_WIDTH), row),
                 pl.BlockSpec((1, POOL_HALO, POOL_WIDTH), lambda bi, i: (bi, jnp.maximum(i * hb - 1, 0), 0)),
                 pl.BlockSpec((1, POOL_HALO, POOL_WIDTH),
                              lambda bi, i: (bi, jnp.minimum((i + 1) * hb, n_halo - 1), 0)),
                 pl.BlockSpec(pool_bd.shape, const), pl.BlockSpec((1, POOL_WIDTH), const),
                 pl.BlockSpec((1, tm, FOURIER_WIDTH), row)]
                + [pl.BlockSpec((1, tm, V_DIM), row)] * ATTN_HEADS
                + [resident((d, d)), pl.BlockSpec((1, d), const),
                   resident((d, d_ff)), resident((d, d_ff)), resident((d_ff, d))])
    args = [x, a, a, a, pool_bd, pool_scale.reshape(1, POOL_WIDTH), y_four, *y_heads, w_out,
            g.reshape(1, d), wg, wu, wd]
    if final_g is not None:
        in_specs.append(pl.BlockSpec((1, d), const))
        args.append(final_g.reshape(1, d))
    return pl.pallas_call(
        functools.partial(_mix_out_ffn_kernel, ff_chunk=ff_chunk, final=final_g is not None),
        grid=(b, s // tm),
        in_specs=in_specs,
        out_specs=pl.BlockSpec((1, tm, d), row),
        out_shape=jax.ShapeDtypeStruct((b, s, d), _F32),
        scratch_shapes=[pltpu.VMEM((tm + 2 * POOL_HALO, POOL_WIDTH), _F32)],
        compiler_params=_params("parallel", "parallel"),
        name="mix_out_ffn",
    )(*args)


def kernel(x, ffn1_norm, ffn1_w_gate, ffn1_w_up, ffn1_w_down, mix_norm, w_in, pool_w, pool_scale, fourier_w, lam_q1, lam_k1, lam_q2, lam_k2, attn_head_norm, w_out, ffn2_norm, ffn2_w_gate, ffn2_w_up, ffn2_w_down, final_norm):
    b, s, d = x.shape
    depth = w_in.shape[0]
    bf = lambda w: w.astype(_BF16)
    (chan_hi, chan_lo), (m1_hi, m1_lo), (m2_hi, m2_lo), tw_cos, tw_sin = _dft_tables(s)
    alibi = [2.0 ** (-8.0 * (i + 1) / ATTN_HEADS) for i in range(ATTN_HEADS)]
    assert all(math.frexp(v)[0] == 0.5 for v in alibi), "position features rely on power-of-two slopes"
    assert POOL_WIDTH + FOURIER_WIDTH == ATTN_WIDTH
    assert all(w & (w - 1) == 0 and w <= 2 * POOL_HALO for w in POOL_WINDOWS) and POOL_WIDTH == 2 * LANES
    ffn1 = (bf(ffn1_w_gate), bf(ffn1_w_up), bf(ffn1_w_down))
    ffn2 = (bf(ffn2_w_gate), bf(ffn2_w_up), bf(ffn2_w_down))
    w_in_b, w_out_b, fourier_b = bf(w_in), bf(w_out), bf(fourier_w)
    w_vt_b = bf(jnp.swapaxes(w_in[:, :, 3 * ATTN_WIDTH:], 1, 2))

    for l in range(depth):
        x, a, z, q, k, vt = _ffn_mix_in(x, ffn1_norm[l], *ffn1, mix_norm[l], w_in_b, w_vt_b, l, chan_hi, chan_lo)
        y_four = _dft_stage2(_dft_stage1(z, m1_hi, m1_lo, tw_cos, tw_sin), m2_hi, m2_lo, fourier_b, l)
        lam_init = jnp.full((1,), 0.8 - 0.6 * math.exp(-0.3 * l), _F32)
        lam_params = jnp.stack([lam_q1[l], lam_k1[l], lam_q2[l], lam_k2[l]]).astype(_F32)
        y_heads = [_diff_attention_head(q, k, vt, h, alibi[h], lam_init, lam_params, attn_head_norm[l])
                   for h in range(ATTN_HEADS)]
        pool_bd = jax.scipy.linalg.block_diag(*[pool_w[l, g] for g in range(len(POOL_WINDOWS))])
        x = _mix_out_ffn(x, a, bf(pool_bd), pool_scale[l], y_four, y_heads, w_out_b, ffn2_norm[l], *ffn2, l,
                         final_norm if l == depth - 1 else None)
    return x
```

```python
import functools
import math

import numpy as np
import jax
import jax.numpy as jnp
from jax import lax
from jax.experimental import pallas as pl
from jax.experimental.pallas import tpu as pltpu

_F32 = jnp.float32
_BF16 = jnp.bfloat16

NORM_EPS = 1e-6
POOL_WIDTH = 256
FOURIER_WIDTH = 256
ATTN_WIDTH = 512
POOL_WINDOWS = (2, 4, 8, 16)
GROUP_DIM = 64
ATTN_HEADS = 4
HEAD_DIM = 64
V_DIM = 2 * HEAD_DIM
POOL_HALO = 8
DFT_L = 128
LANES = 128
LOG2_E = math.log2(math.e)
FLUSH_EXP = 127.0
FRAME_SLACK = 1.0 + 2.0 ** -5
NO_TILE = 1e30
MIN_DENOMINATOR = 2.0 ** -90

_VMEM_LIMIT = 56 * 1024 * 1024


def _params(*semantics, flags=None):
    return pltpu.CompilerParams(dimension_semantics=semantics, vmem_limit_bytes=_VMEM_LIMIT, flags=flags)


def _rms_scale(x, g):
    ms = jnp.mean(x * x, axis=-1, keepdims=True)
    return x * lax.rsqrt(ms + NORM_EPS) * g


def _split_hi_lo(x):
    hi = x.astype(_BF16)
    lo = (x - hi.astype(_F32)).astype(_BF16)
    return hi, lo


def _dot3(a_hi, a_lo, b_hi, b_lo):
    d = functools.partial(jnp.dot, preferred_element_type=_F32)
    return d(a_hi, b_hi) + (d(a_lo, b_hi) + d(a_hi, b_lo))


def _swiglu_half_step(x, g_ref, wg_ref, wu_ref, wd_ref, ff_chunk):
    hn = _rms_scale(x, g_ref[...]).astype(_BF16)
    acc = jnp.zeros(x.shape, _F32)
    for c in range(wg_ref.shape[1] // ff_chunk):
        sl = slice(c * ff_chunk, (c + 1) * ff_chunk)
        gate = jnp.dot(hn, wg_ref[:, sl], preferred_element_type=_F32)
        up = jnp.dot(hn, wu_ref[:, sl], preferred_element_type=_F32)
        act = (gate / (1.0 + jnp.exp(-gate)) * up).astype(_BF16)
        acc = acc + jnp.dot(act, wd_ref[sl, :], preferred_element_type=_F32)
    return x + 0.5 * acc


def _ffn_mix_in_kernel(x_ref, g1_ref, wg_ref, wu_ref, wd_ref, g_ref, waf_ref, wq_ref, wk_ref, wvt_ref,
                       chi_ref, clo_ref, x_out_ref, a_ref, z_ref, q_ref, k_ref, vt_ref, *, ff_chunk):
    x = _swiglu_half_step(x_ref[0], g1_ref, wg_ref, wu_ref, wd_ref, ff_chunk)
    x_out_ref[0] = x
    hn = _rms_scale(x, g_ref[...]).astype(_BF16)
    af = jnp.dot(hn, waf_ref[...], preferred_element_type=_F32)
    a_ref[0] = af[:, :POOL_WIDTH]
    f_hi, f_lo = _split_hi_lo(af[:, POOL_WIDTH:])
    z_ref[0] = _dot3(f_hi, f_lo, chi_ref[...], clo_ref[...])
    q = jnp.dot(hn, wq_ref[...], preferred_element_type=_F32)
    q_ref[0] = (q * (HEAD_DIM ** -0.5 * LOG2_E)).astype(_BF16)
    k_ref[0] = jnp.dot(hn, wk_ref[...], preferred_element_type=_F32).astype(_BF16)
    vt = lax.dot_general(wvt_ref[...], hn, (((1,), (1,)), ((), ())), preferred_element_type=_F32)
    vt_ref[0] = vt.astype(_BF16)


def _ffn_mix_in(x, g1, wg, wu, wd, g, w_in, w_vt, layer, chan_hi, chan_lo, *, tm=512, ff_chunk=256):
    b, s, d = x.shape
    d_ff = wg.shape[2]
    const = lambda bi, i: (0, 0)
    row = lambda bi, i: (bi, i, 0)
    full = lambda arr: pl.BlockSpec(arr.shape, const)
    once = pl.Buffered(1)
    resident = lambda shape: pl.BlockSpec((None,) + shape, lambda bi, i: (layer, 0, 0), pipeline_mode=once)
    cols = lambda j: pl.BlockSpec((None, d, ATTN_WIDTH), lambda bi, i: (layer, 0, j), pipeline_mode=once)
    return pl.pallas_call(
        functools.partial(_ffn_mix_in_kernel, ff_chunk=ff_chunk),
        grid=(b, s // tm),
        in_specs=[pl.BlockSpec((1, tm, d), row), pl.BlockSpec((1, d), const),
                  resident((d, d_ff)), resident((d, d_ff)), resident((d_ff, d)),
                  pl.BlockSpec((1, d), const), cols(0), cols(1), cols(2), resident(w_vt.shape[1:]),
                  full(chan_hi), full(chan_lo)],
        out_specs=[pl.BlockSpec((1, tm, d), row),
                   pl.BlockSpec((1, tm, POOL_WIDTH), row),
                   pl.BlockSpec((1, tm, 2 * FOURIER_WIDTH), row),
                   pl.BlockSpec((1, tm, ATTN_WIDTH), row),
                   pl.BlockSpec((1, tm, ATTN_WIDTH), row),
                   pl.BlockSpec((1, ATTN_WIDTH, tm), lambda bi, i: (bi, 0, i))],
        out_shape=[jax.ShapeDtypeStruct((b, s, d), _F32),
                   jax.ShapeDtypeStruct((b, s, POOL_WIDTH), _F32),
                   jax.ShapeDtypeStruct((b, s, 2 * FOURIER_WIDTH), _F32),
                   jax.ShapeDtypeStruct((b, s, ATTN_WIDTH), _BF16),
                   jax.ShapeDtypeStruct((b, s, ATTN_WIDTH), _BF16),
                   jax.ShapeDtypeStruct((b, ATTN_WIDTH, s), _BF16)],
        compiler_params=_params("parallel", "parallel"),
        name="ffn_mix_in",
    )(x, g1.reshape(1, d), wg, wu, wd, g.reshape(1, d), w_in, w_in, w_in, w_vt, chan_hi, chan_lo)


def _dft_tables(s):
    h, l = s // DFT_L, DFT_L
    two_pi = 2.0 * np.pi
    c = np.arange(GROUP_DIM)
    ang = two_pi * np.outer(c, c) / GROUP_DIM
    eye = np.eye(FOURIER_WIDTH // GROUP_DIM)
    chan = np.concatenate([np.kron(eye, np.cos(ang)), np.kron(eye, -np.sin(ang))], axis=1)
    u = np.arange(h)
    ang1 = two_pi * np.outer(u, u) / h
    stage1 = np.concatenate([np.cos(ang1), -np.sin(ang1)], axis=0)
    t = np.arange(l)
    angt = two_pi * np.outer(u, t) / s
    tw_cos, tw_sin = np.cos(angt), np.sin(angt)
    ang2 = two_pi * np.outer(t, t) / l
    scale = 1.0 / math.sqrt(s * GROUP_DIM)
    stage2 = np.concatenate([np.cos(ang2), np.sin(ang2)], axis=1) * scale
    as_f32 = lambda a: jnp.asarray(a, _F32)
    return (_split_hi_lo(as_f32(chan)), _split_hi_lo(as_f32(stage1)), _split_hi_lo(as_f32(stage2)),
            jnp.repeat(as_f32(tw_cos), LANES, axis=1), jnp.repeat(as_f32(tw_sin), LANES, axis=1))


def _dft_stage1_kernel(z_ref, mhi_ref, mlo_ref, twc_ref, tws_ref, o_ref, *, slabs):
    h = z_ref.shape[1]
    w = FOURIER_WIDTH
    z = jnp.concatenate([z_ref[0, :, j, :] for j in range(slabs)], axis=1)
    z_hi, z_lo = _split_hi_lo(z)
    p = _dot3(mhi_ref[...], mlo_ref[...], z_hi, z_lo)
    for j in range(slabs):
        re, im = slice(2 * j * w, (2 * j + 1) * w), slice((2 * j + 1) * w, (2 * j + 2) * w)
        ar = p[:h, re] - p[h:, im]
        ai = p[:h, im] + p[h:, re]
        tw = slice(j * LANES, (j + 1) * LANES)
        tc = jnp.concatenate([twc_ref[:, tw]] * (w // LANES), axis=1)
        ts = jnp.concatenate([tws_ref[:, tw]] * (w // LANES), axis=1)
        o_ref[0, :, j, :] = jnp.concatenate([ar * tc + ai * ts, ai * tc - ar * ts], axis=1)


def _dft_stage1(z, m_hi, m_lo, tw_cos, tw_sin, *, slabs=16):
    b, s, zw = z.shape
    h = s // DFT_L
    z4 = z.reshape(b, h, DFT_L, zw)
    const = lambda bi, j: (0, 0)
    slab_block = pl.BlockSpec((1, h, slabs, zw), lambda bi, j: (bi, 0, j, 0))
    return pl.pallas_call(
        functools.partial(_dft_stage1_kernel, slabs=slabs),
        grid=(b, DFT_L // slabs),
        in_specs=[slab_block,
                  pl.BlockSpec(m_hi.shape, const), pl.BlockSpec(m_lo.shape, const),
                  pl.BlockSpec((h, slabs * LANES), lambda bi, j: (0, j)),
                  pl.BlockSpec((h, slabs * LANES), lambda bi, j: (0, j))],
        out_specs=slab_block,
        out_shape=jax.ShapeDtypeStruct(z4.shape, _F32),
        compiler_params=_params("parallel", "parallel"),
        name="dft_stage1",
    )(z4, m_hi, m_lo, tw_cos, tw_sin)


def _dft_stage2_kernel(a_ref, mhi_ref, mlo_ref, fw_ref, o_ref, *, rows):
    w = FOURIER_WIDTH
    for r in range(rows):
        blk = a_ref[0, r]
        stacked = jnp.concatenate([blk[:, :w], blk[:, w:]], axis=0)
        s_hi, s_lo = _split_hi_lo(stacked)
        y = _dot3(mhi_ref[...], mlo_ref[...], s_hi, s_lo)
        o_ref[0, :, r, :] = jnp.dot(y.astype(_BF16), fw_ref[...], preferred_element_type=_F32)


def _dft_stage2(a4, m_hi, m_lo, fourier_w, layer, *, rows=8):
    b, h, l, zw = a4.shape
    w = FOURIER_WIDTH
    const = lambda bi, j: (0, 0)
    out = pl.pallas_call(
        functools.partial(_dft_stage2_kernel, rows=rows),
        grid=(b, h // rows),
        in_specs=[pl.BlockSpec((1, rows, l, zw), lambda bi, j: (bi, j, 0, 0)),
                  pl.BlockSpec(m_hi.shape, const), pl.BlockSpec(m_lo.shape, const),
                  pl.BlockSpec((None,) + fourier_w.shape[1:], lambda bi, j: (layer, 0, 0))],
        out_specs=pl.BlockSpec((1, l, rows, w), lambda bi, j: (bi, 0, j, 0)),
        out_shape=jax.ShapeDtypeStruct((b, l, h, w), _F32),
        compiler_params=_params("parallel", "parallel"),
        name="dft_stage2",
    )(a4, m_hi, m_lo, fourier_w)
    return out.reshape(b, l * h, w)


def _bias_feature_tables(tile):
    pos = np.arange(tile)
    hi, lo = (pos // LANES) * LANES, pos % LANES
    qf = np.zeros((tile, LANES), np.float32)
    kf = np.zeros((tile, LANES), np.float32)
    rest = LOG2_E
    for p in range(3):
        part = float(np.float32(rest).astype(jnp.bfloat16).astype(np.float32))
        rest -= part
        qf[:, 2 * p], qf[:, 2 * p + 1] = hi, lo
        kf[:, 2 * p], kf[:, 2 * p + 1] = part, part
        qf[:, 6 + 2 * p], qf[:, 7 + 2 * p] = -part, -part
        kf[:, 6 + 2 * p], kf[:, 7 + 2 * p] = hi, lo
    return jnp.asarray(qf, _BF16), jnp.asarray(np.stack([-kf, 0.0 * kf, kf]), _BF16)


def _attn_kernel(lam_init_ref, lam_ref, gain_ref, qft_ref, kf_ref, q_ref, k_ref, vt_ref, o_ref,
                 w_ref, acc_ref, p_ref, knorm_ref, *, slope, reach, q_tiles):
    t = q_ref.shape[1] // q_tiles
    n = k_ref.shape[1] // t
    step = pl.program_id(1)
    c = slope * LOG2_E
    nt = (((1,), (1,)), ((), ()))
    half_rows = (lax.broadcasted_iota(jnp.int32, (8, LANES), 1) // HEAD_DIM
                 == lax.broadcasted_iota(jnp.int32, (8, LANES), 0)).astype(_F32)

    def half_norms(x):
        sq = jnp.square(x.astype(_F32))
        return jnp.sqrt(lax.dot_general(half_rows, sq, nt, preferred_element_type=_F32)[0:2])

    @pl.when(step == 0)
    def _():
        knorm_ref[0] = jnp.max(half_norms(k_ref[0]))

    def explicit_bias(tile_delta):
        dist = jnp.abs(lax.broadcasted_iota(jnp.int32, (t, t), 0) - lax.broadcasted_iota(jnp.int32, (t, t), 1)
                       + tile_delta * t)
        bias = dist.astype(_F32) * (-c)
        return jnp.concatenate([bias, bias], axis=1)

    def pv(tile_idx, p_buf):
        k0 = pl.multiple_of(tile_idx * t, t)
        return jnp.dot(vt_ref[0, :, pl.ds(k0, t)], p_buf[...], preferred_element_type=_F32)

    def query_tile(u):
        qi = step * q_tiles + u
        rows = slice(u * t, (u + 1) * t)
        w_u, acc_u, p_a, p_b = w_ref.at[u], acc_ref.at[u], p_ref.at[2 * u], p_ref.at[2 * u + 1]

        qh = q_ref[0, rows, :]
        q_t = qh.astype(_F32).T
        row = lax.broadcasted_iota(jnp.int32, q_t.shape, 0)
        zero = jnp.zeros_like(q_t)
        halves = jnp.concatenate([jnp.where(row < HEAD_DIM, q_t, zero), jnp.where(row >= HEAD_DIM, q_t, zero)],
                                 axis=1)
        qf_t = qft_ref[...].astype(_F32) * slope
        w_u[...] = jnp.concatenate([halves, jnp.concatenate([qf_t, qf_t], axis=1)], axis=0).astype(_BF16)
        qn = [jnp.sqrt(jnp.sum(jnp.square(q_t[j * HEAD_DIM:(j + 1) * HEAD_DIM]), axis=0, keepdims=True))
              for j in range(2)]
        frame = jnp.concatenate(qn, axis=1) * (knorm_ref[0] * FRAME_SLACK) + 0.5

        def qk(tile_idx, side):
            k0 = pl.multiple_of(tile_idx * t, t)
            lhs = jnp.concatenate([k_ref[0, pl.ds(k0, t), :], kf_ref[side]], axis=1)
            return jnp.dot(lhs, w_u[...], preferred_element_type=_F32)

        def finish(o_all):
            lam_init = lam_init_ref[0]
            lp = lam_ref[...]
            lam = (jnp.exp(jnp.sum(lp[0:1] * lp[1:2], axis=-1, keepdims=True))
                   - jnp.exp(jnp.sum(lp[2:3] * lp[3:4], axis=-1, keepdims=True)) + lam_init)
            o = (o_all[:, :t] - lam * o_all[:, t:]).T
            o = _rms_scale(o, gain_ref[...]) * (1.0 - lam_init)
            o_ref[0, rows, :] = o.astype(o_ref.dtype)

        if 2 * reach + 1 >= n:
            others = [j + (qi <= j).astype(jnp.int32) for j in range(n - 1)]
            visits = [(qi, 1, None)] + [(i, jnp.where(i < qi, 0, 2), jnp.abs(i - qi).astype(_F32) * (-c * t))
                                        for i in others]
        else:
            visits = [(qi, 1, None)]
            for d in [sign * dist for dist in range(1, reach + 1) for sign in (-1, 1)]:
                i = qi + d
                inside = jnp.logical_and(i >= 0, i < n)
                visits.append((jnp.clip(i, 0, n - 1), 0 if d < 0 else 2,
                               jnp.where(inside, -c * t * abs(d), -NO_TILE)))

        acc_u[...] = jnp.zeros(acc_u.shape, _F32)
        bufs = (p_a, p_b)
        l = jnp.zeros((1, 2 * t), _F32)
        for v, (i, side, off) in enumerate(visits):
            if off is None:
                p = jnp.exp2(qk(i, side) + explicit_bias(0) - frame)
            else:
                p = jnp.exp2(qk(i, side) - (frame - off))
            l = l + jnp.sum(p, axis=0, keepdims=True)
            bufs[v % 2][...] = p.astype(_BF16)
            if v > 0:
                acc_u[...] += pv(visits[v - 1][0], bufs[(v - 1) % 2])
        last = len(visits) - 1
        o_all = (acc_u[...] + pv(visits[last][0], bufs[last % 2])) / l
        finish(o_all)
        unusable = jnp.logical_or(jnp.logical_not(jnp.sum(o_all * 0.0) == 0.0), jnp.min(l) < MIN_DENOMINATOR)

        def recompute():
            acc_u[...] = jnp.zeros(acc_u.shape, _F32)

            def safe_body(i, carry):
                m, l_run = carry
                sc = qk(i, 1) + explicit_bias(i - qi)
                m_new = jnp.maximum(m, jnp.max(sc, axis=0, keepdims=True))
                alpha = jnp.exp2(m - m_new)
                p = jnp.exp2(sc - m_new)
                p_a[...] = p.astype(_BF16)
                acc_u[...] = alpha * acc_u[...] + pv(i, p_a)
                return m_new, alpha * l_run + jnp.sum(p, axis=0, keepdims=True)

            init = (jnp.full((1, 2 * t), -jnp.inf, _F32), jnp.zeros((1, 2 * t), _F32))
            _, l_safe = lax.fori_loop(0, n, safe_body, init)
            finish(acc_u[...] / l_safe)

        return unusable, recompute

    for unusable, recompute in [query_tile(u) for u in range(q_tiles)]:
        pl.when(unusable)(recompute)


def _diff_attention_head(q, k, vt, head, slope, lam_init, lam_params, head_gain, *, tile=512):
    b, s, _ = q.shape
    n = s // tile
    qf, kf = _bias_feature_tables(tile)
    qf_t = qf.T
    reach = int(math.floor((FLUSH_EXP / (slope * LOG2_E) - 1) / tile)) + 1
    q_tiles = 2 if 2 * reach + 1 >= n else 4
    assert n >= 2 and n % q_tiles == 0
    smem = pl.BlockSpec(memory_space=pltpu.SMEM)
    const = lambda shape: pl.BlockSpec(shape, lambda bi, qi: (0,) * len(shape))
    return pl.pallas_call(
        functools.partial(_attn_kernel, slope=slope, reach=reach, q_tiles=q_tiles),
        grid=(b, n // q_tiles),
        in_specs=[smem, const(lam_params.shape),
                  pl.BlockSpec((1, V_DIM), lambda bi, qi: (0, head)),
                  const(qf_t.shape), const(kf.shape),
                  pl.BlockSpec((1, q_tiles * tile, V_DIM), lambda bi, qi: (bi, qi, head)),
                  pl.BlockSpec((1, s, V_DIM), lambda bi, qi: (bi, 0, head)),
                  pl.BlockSpec((1, V_DIM, s), lambda bi, qi: (bi, head, 0))],
        out_specs=pl.BlockSpec((1, q_tiles * tile, V_DIM), lambda bi, qi: (bi, qi, 0)),
        out_shape=jax.ShapeDtypeStruct((b, s, V_DIM), _BF16),
        scratch_shapes=[pltpu.VMEM((q_tiles, 2 * LANES, 2 * tile), _BF16),
                        pltpu.VMEM((q_tiles, V_DIM, 2 * tile), _F32),
                        pltpu.VMEM((2 * q_tiles, tile, 2 * tile), _BF16),
                        pltpu.SMEM((1,), _F32)],
        compiler_params=_params("arbitrary", "arbitrary"),
        name=f"diff_attn_h{head}",
    )(lam_init, lam_params, head_gain.reshape(1, ATTN_WIDTH), qf_t, kf, q, k, vt)


def _mix_out_ffn_kernel(x_ref, a_ref, a_prev_ref, a_next_ref, pw_ref, ps_ref, yf_ref, *rest, ff_chunk, final):
    head_refs, rest = rest[:ATTN_HEADS], rest[ATTN_HEADS:]
    if final:
        wo_ref, g_ref, wg_ref, wu_ref, wd_ref, gf_ref, o_ref, buf_ref = rest
    else:
        wo_ref, g_ref, wg_ref, wu_ref, wd_ref, o_ref, buf_ref = rest
    i = pl.program_id(1)
    tm = a_ref.shape[1]
    s_len = tm * pl.num_programs(1)
    halo = POOL_HALO
    buf_ref[0:halo] = jnp.where(i > 0, a_prev_ref[0], 0.0)
    buf_ref[halo:halo + tm] = a_ref[0]
    buf_ref[halo + tm:] = jnp.where(i < pl.num_programs(1) - 1, a_next_ref[0], 0.0)
    lane = lax.broadcasted_iota(jnp.int32, (1, LANES), 1)
    pick = lambda v0, v1: jnp.where(lane >= GROUP_DIM, v1, v0)

    def pooled(lanes, windows):
        shifted = lambda d: buf_ref[halo + d:halo + d + tm, lanes]
        a = shifted(0)
        sums, total, lo, hi = [], a, 0, 0
        for w in windows:
            left, right = w // 2, w - 1 - w // 2
            for d in list(range(-left, lo)) + list(range(hi + 1, right + 1)):
                total = total + shifted(d)
            lo, hi = -left, right
            sums.append(total)
        win = pick(*sums)
        mixed = win * pick(*[1.0 / w for w in windows]) - a

        def edge(rows):
            t = i * tm + rows.start + lax.broadcasted_iota(jnp.int32, (halo, LANES), 0)
            left, right = pick(*[w // 2 for w in windows]), pick(*[w - 1 - w // 2 for w in windows])
            cnt = jnp.minimum(t + right + 1, s_len) - jnp.maximum(t - left, 0)
            return win[rows] / cnt.astype(_F32) - a[rows]

        return jnp.concatenate([edge(slice(0, halo)), mixed[halo:tm - halo], edge(slice(tm - halo, tm))], axis=0)

    mixed = jnp.concatenate([pooled(slice(0, LANES), POOL_WINDOWS[:2]),
                             pooled(slice(LANES, 2 * LANES), POOL_WINDOWS[2:])], axis=1)
    y_pool = jnp.dot(mixed.astype(_BF16), pw_ref[...], preferred_element_type=_F32) * ps_ref[...]

    rest = jnp.concatenate([yf_ref[0].astype(_BF16)] + [r[0] for r in head_refs], axis=1)
    y = jnp.dot(rest, wo_ref[POOL_WIDTH:, :], preferred_element_type=_F32)
    y = y + jnp.dot(y_pool.astype(_BF16), wo_ref[:POOL_WIDTH, :], preferred_element_type=_F32)
    x = _swiglu_half_step(x_ref[0] + y, g_ref, wg_ref, wu_ref, wd_ref, ff_chunk)
    o_ref[0] = _rms_scale(x, gf_ref[...]) if final else x


def _mix_out_ffn(x, a, pool_bd, pool_scale, y_four, y_heads, w_out, g, wg, wu, wd, layer, final_g=None,
                 *, tm=512, ff_chunk=256):
    b, s, d = x.shape
    d_ff = wg.shape[2]
    hb = tm // POOL_HALO
    n_halo = s // POOL_HALO
    row = lambda bi, i: (bi, i, 0)
    const = lambda bi, i: (0, 0)
    resident = lambda shape: pl.BlockSpec((None,) + shape, lambda bi, i: (layer, 0, 0),
                                          pipeline_mode=pl.Buffered(1))
    in_specs = ([pl.BlockSpec((1, tm, d), row),
                 pl.BlockSpec((1, tm, POOL_WIDTH), row),
                 pl.BlockSpec((1, POOL_HALO, POOL_WIDTH), lambda bi, i: (bi, jnp.maximum(i * hb - 1, 0), 0)),
                 pl.BlockSpec((1, POOL_HALO, POOL_WIDTH),
                              lambda bi, i: (bi, jnp.minimum((i + 1) * hb, n_halo - 1), 0)),
                 pl.BlockSpec(pool_bd.shape, const), pl.BlockSpec((1, POOL_WIDTH), const),
                 pl.BlockSpec((1, tm, FOURIER_WIDTH), row)]
                + [pl.BlockSpec((1, tm, V_DIM), row)] * ATTN_HEADS
                + [resident((d, d)), pl.BlockSpec((1, d), const),
                   resident((d, d_ff)), resident((d, d_ff)), resident((d_ff, d))])
    args = [x, a, a, a, pool_bd, pool_scale.reshape(1, POOL_WIDTH), y_four, *y_heads, w_out,
            g.reshape(1, d), wg, wu, wd]
    if final_g is not None:
        in_specs.append(pl.BlockSpec((1, d), const))
        args.append(final_g.reshape(1, d))
    return pl.pallas_call(
        functools.partial(_mix_out_ffn_kernel, ff_chunk=ff_chunk, final=final_g is not None),
        grid=(b, s // tm),
        in_specs=in_specs,
        out_specs=pl.BlockSpec((1, tm, d), row),
        out_shape=jax.ShapeDtypeStruct((b, s, d), _F32),
        scratch_shapes=[pltpu.VMEM((tm + 2 * POOL_HALO, POOL_WIDTH), _F32)],
        compiler_params=_params("parallel", "parallel"),
        name="mix_out_ffn",
    )(*args)


def kernel(x, ffn1_norm, ffn1_w_gate, ffn1_w_up, ffn1_w_down, mix_norm, w_in, pool_w, pool_scale, fourier_w, lam_q1, lam_k1, lam_q2, lam_k2, attn_head_norm, w_out, ffn2_norm, ffn2_w_gate, ffn2_w_up, ffn2_w_down, final_norm):
    b, s, d = x.shape
    depth = w_in.shape[0]
    bf = lambda w: w.astype(_BF16)
    (chan_hi, chan_lo), (m1_hi, m1_lo), (m2_hi, m2_lo), tw_cos, tw_sin = _dft_tables(s)
    alibi = [2.0 ** (-8.0 * (i + 1) / ATTN_HEADS) for i in range(ATTN_HEADS)]
    assert all(math.frexp(v)[0] == 0.5 for v in alibi), "position features rely on power-of-two slopes"
    assert POOL_WIDTH + FOURIER_WIDTH == ATTN_WIDTH
    assert all(w & (w - 1) == 0 and w <= 2 * POOL_HALO for w in POOL_WINDOWS) and POOL_WIDTH == 2 * LANES
    ffn1 = (bf(ffn1_w_gate), bf(ffn1_w_up), bf(ffn1_w_down))
    ffn2 = (bf(ffn2_w_gate), bf(ffn2_w_up), bf(ffn2_w_down))
    w_in_b, w_out_b, fourier_b = bf(w_in), bf(w_out), bf(fourier_w)
    w_vt_b = bf(jnp.swapaxes(w_in[:, :, 3 * ATTN_WIDTH:], 1, 2))

    for l in range(depth):
        x, a, z, q, k, vt = _ffn_mix_in(x, ffn1_norm[l], *ffn1, mix_norm[l], w_in_b, w_vt_b, l, chan_hi, chan_lo)
        y_four = _dft_stage2(_dft_stage1(z, m1_hi, m1_lo, tw_cos, tw_sin), m2_hi, m2_lo, fourier_b, l)
        lam_init = jnp.full((1,), 0.8 - 0.6 * math.exp(-0.3 * l), _F32)
        lam_params = jnp.stack([lam_q1[l], lam_k1[l], lam_q2[l], lam_k2[l]]).astype(_F32)
        y_heads = [_diff_attention_head(q, k, vt, h, alibi[h], lam_init, lam_params, attn_head_norm[l])
                   for h in range(ATTN_HEADS)]
        pool_bd = jax.scipy.linalg.block_diag(*[pool_w[l, g] for g in range(len(POOL_WINDOWS))])
        x = _mix_out_ffn(x, a, bf(pool_bd), pool_scale[l], y_four, y_heads, w_out_b, ffn2_norm[l], *ffn2, l,
                         final_norm if l == depth - 1 else None)
    return x
```

```python
import functools
import math

import numpy as np
import jax
import jax.numpy as jnp
from jax import lax
from jax.experimental import pallas as pl
from jax.experimental.pallas import tpu as pltpu

_F32 = jnp.float32
_BF16 = jnp.bfloat16

NORM_EPS = 1e-6
POOL_WIDTH = 256
FOURIER_WIDTH = 256
ATTN_WIDTH = 512
POOL_WINDOWS = (2, 4, 8, 16)
GROUP_DIM = 64
ATTN_HEADS = 4
HEAD_DIM = 64
V_DIM = 2 * HEAD_DIM
POOL_HALO = 8
DFT_L = 128
LANES = 128
LOG2_E = math.log2(math.e)
FLUSH_EXP = 127.0
FRAME_SLACK = 1.0 + 2.0 ** -5
NO_TILE = 1e30
MIN_DENOMINATOR = 2.0 ** -90

_VMEM_LIMIT = 56 * 1024 * 1024


def _params(*semantics, flags=None):
    return pltpu.CompilerParams(dimension_semantics=semantics, vmem_limit_bytes=_VMEM_LIMIT, flags=flags)


def _rms_scale(x, g):
    ms = jnp.mean(x * x, axis=-1, keepdims=True)
    return x * lax.rsqrt(ms + NORM_EPS) * g


def _split_hi_lo(x):
    hi = x.astype(_BF16)
    lo = (x - hi.astype(_F32)).astype(_BF16)
    return hi, lo


def _dot3(a_hi, a_lo, b_hi, b_lo):
    d = functools.partial(jnp.dot, preferred_element_type=_F32)
    return d(a_hi, b_hi) + (d(a_lo, b_hi) + d(a_hi, b_lo))


def _dot3_small_lhs(a_hi, a_lo, b_hi, b_lo):
    d = functools.partial(jnp.dot, preferred_element_type=_F32)
    m = a_hi.shape[0]
    both = d(jnp.concatenate([a_hi, a_lo], axis=0), b_hi)
    return both[:m] + (both[m:] + d(a_hi, b_lo))


def _swiglu_half_step(x, g_ref, wg_ref, wu_ref, wd_ref, ff_chunk):
    hn = _rms_scale(x, g_ref[...]).astype(_BF16)
    acc = jnp.zeros(x.shape, _F32)
    for c in range(wg_ref.shape[1] // ff_chunk):
        sl = slice(c * ff_chunk, (c + 1) * ff_chunk)
        gate = jnp.dot(hn, wg_ref[:, sl], preferred_element_type=_F32)
        up = jnp.dot(hn, wu_ref[:, sl], preferred_element_type=_F32)
        act = (gate / (1.0 + jnp.exp(-gate)) * up).astype(_BF16)
        acc = acc + jnp.dot(act, wd_ref[sl, :], preferred_element_type=_F32)
    return x + 0.5 * acc


def _ffn_mix_in_kernel(x_ref, g1_ref, wg_ref, wu_ref, wd_ref, g_ref, waf_ref, wq_ref, wk_ref, wvt_ref,
                       chi_ref, clo_ref, x_out_ref, a_ref, z_ref, q_ref, k_ref, vt_ref, *, ff_chunk):
    x = _swiglu_half_step(x_ref[0], g1_ref, wg_ref, wu_ref, wd_ref, ff_chunk)
    x_out_ref[0] = x
    hn = _rms_scale(x, g_ref[...]).astype(_BF16)
    af = jnp.dot(hn, waf_ref[...], preferred_element_type=_F32)
    a_ref[0] = af[:, :POOL_WIDTH]
    f_hi, f_lo = _split_hi_lo(af[:, POOL_WIDTH:])
    z_ref[0] = _dot3(f_hi, f_lo, chi_ref[...], clo_ref[...])
    q = jnp.dot(hn, wq_ref[...], preferred_element_type=_F32)
    q_ref[0] = (q * (HEAD_DIM ** -0.5 * LOG2_E)).astype(_BF16)
    k_ref[0] = jnp.dot(hn, wk_ref[...], preferred_element_type=_F32).astype(_BF16)
    vt = lax.dot_general(wvt_ref[...], hn, (((1,), (1,)), ((), ())), preferred_element_type=_F32)
    vt_ref[0] = vt.astype(_BF16)


def _ffn_mix_in(x, g1, wg, wu, wd, g, w_in, w_vt, layer, chan_hi, chan_lo, *, tm=512, ff_chunk=256):
    b, s, d = x.shape
    d_ff = wg.shape[2]
    const = lambda bi, i: (0, 0)
    row = lambda bi, i: (bi, i, 0)
    full = lambda arr: pl.BlockSpec(arr.shape, const)
    once = pl.Buffered(1)
    resident = lambda shape: pl.BlockSpec((None,) + shape, lambda bi, i: (layer, 0, 0), pipeline_mode=once)
    cols = lambda j: pl.BlockSpec((None, d, ATTN_WIDTH), lambda bi, i: (layer, 0, j), pipeline_mode=once)
    return pl.pallas_call(
        functools.partial(_ffn_mix_in_kernel, ff_chunk=ff_chunk),
        grid=(b, s // tm),
        in_specs=[pl.BlockSpec((1, tm, d), row), pl.BlockSpec((1, d), const),
                  resident((d, d_ff)), resident((d, d_ff)), resident((d_ff, d)),
                  pl.BlockSpec((1, d), const), cols(0), cols(1), cols(2), resident(w_vt.shape[1:]),
                  full(chan_hi), full(chan_lo)],
        out_specs=[pl.BlockSpec((1, tm, d), row),
                   pl.BlockSpec((1, tm, POOL_WIDTH), row),
                   pl.BlockSpec((1, tm, 2 * FOURIER_WIDTH), row),
                   pl.BlockSpec((1, tm, ATTN_WIDTH), row),
                   pl.BlockSpec((1, tm, ATTN_WIDTH), row),
                   pl.BlockSpec((1, ATTN_WIDTH, tm), lambda bi, i: (bi, 0, i))],
        out_shape=[jax.ShapeDtypeStruct((b, s, d), _F32),
                   jax.ShapeDtypeStruct((b, s, POOL_WIDTH), _F32),
                   jax.ShapeDtypeStruct((b, s, 2 * FOURIER_WIDTH), _F32),
                   jax.ShapeDtypeStruct((b, s, ATTN_WIDTH), _BF16),
                   jax.ShapeDtypeStruct((b, s, ATTN_WIDTH), _BF16),
                   jax.ShapeDtypeStruct((b, ATTN_WIDTH, s), _BF16)],
        compiler_params=_params("parallel", "parallel"),
        name="ffn_mix_in",
    )(x, g1.reshape(1, d), wg, wu, wd, g.reshape(1, d), w_in, w_in, w_in, w_vt, chan_hi, chan_lo)


def _dft_tables(s):
    h, l = s // DFT_L, DFT_L
    two_pi = 2.0 * np.pi
    c = np.arange(GROUP_DIM)
    ang = two_pi * np.outer(c, c) / GROUP_DIM
    eye = np.eye(FOURIER_WIDTH // GROUP_DIM)
    chan = np.concatenate([np.kron(eye, np.cos(ang)), np.kron(eye, -np.sin(ang))], axis=1)
    u = np.arange(h)
    ang1 = two_pi * np.outer(u, u) / h
    stage1 = np.concatenate([np.cos(ang1), -np.sin(ang1)], axis=0)
    t = np.arange(l)
    angt = two_pi * np.outer(u, t) / s
    tw_cos, tw_sin = np.cos(angt), np.sin(angt)
    ang2 = two_pi * np.outer(t, t) / l
    scale = 1.0 / math.sqrt(s * GROUP_DIM)
    stage2 = np.concatenate([np.cos(ang2), np.sin(ang2)], axis=1) * scale
    as_f32 = lambda a: jnp.asarray(a, _F32)
    return (_split_hi_lo(as_f32(chan)), _split_hi_lo(as_f32(stage1)), _split_hi_lo(as_f32(stage2)),
            jnp.repeat(as_f32(tw_cos), LANES, axis=1), jnp.repeat(as_f32(tw_sin), LANES, axis=1))


def _dft_stage1_kernel(z_ref, mhi_ref, mlo_ref, twc_ref, tws_ref, o_ref, *, slabs):
    h = z_ref.shape[1]
    w = FOURIER_WIDTH
    z = jnp.concatenate([z_ref[0, :, j, :] for j in range(slabs)], axis=1)
    z_hi, z_lo = _split_hi_lo(z)
    p = _dot3_small_lhs(mhi_ref[...], mlo_ref[...], z_hi, z_lo)
    for j in range(slabs):
        re, im = slice(2 * j * w, (2 * j + 1) * w), slice((2 * j + 1) * w, (2 * j + 2) * w)
        ar = p[:h, re] - p[h:, im]
        ai = p[:h, im] + p[h:, re]
        tw = slice(j * LANES, (j + 1) * LANES)
        tc = jnp.concatenate([twc_ref[:, tw]] * (w // LANES), axis=1)
        ts = jnp.concatenate([tws_ref[:, tw]] * (w // LANES), axis=1)
        o_ref[0, :, j, :] = jnp.concatenate([ar * tc + ai * ts, ai * tc - ar * ts], axis=1)


def _dft_stage1(z, m_hi, m_lo, tw_cos, tw_sin, *, slabs=16):
    b, s, zw = z.shape
    h = s // DFT_L
    z4 = z.reshape(b, h, DFT_L, zw)
    const = lambda bi, j: (0, 0)
    slab_block = pl.BlockSpec((1, h, slabs, zw), lambda bi, j: (bi, 0, j, 0))
    return pl.pallas_call(
        functools.partial(_dft_stage1_kernel, slabs=slabs),
        grid=(b, DFT_L // slabs),
        in_specs=[slab_block,
                  pl.BlockSpec(m_hi.shape, const), pl.BlockSpec(m_lo.shape, const),
                  pl.BlockSpec((h, slabs * LANES), lambda bi, j: (0, j)),
                  pl.BlockSpec((h, slabs * LANES), lambda bi, j: (0, j))],
        out_specs=slab_block,
        out_shape=jax.ShapeDtypeStruct(z4.shape, _F32),
        compiler_params=_params("parallel", "parallel"),
        name="dft_stage1",
    )(z4, m_hi, m_lo, tw_cos, tw_sin)


def _dft_stage2_kernel(a_ref, mhi_ref, mlo_ref, fw_ref, o_ref, *, rows):
    w = FOURIER_WIDTH
    for r in range(rows):
        blk = a_ref[0, r]
        stacked = jnp.concatenate([blk[:, :w], blk[:, w:]], axis=0)
        s_hi, s_lo = _split_hi_lo(stacked)
        y = _dot3_small_lhs(mhi_ref[...], mlo_ref[...], s_hi, s_lo)
        o_ref[0, :, r, :] = jnp.dot(y.astype(_BF16), fw_ref[...], preferred_element_type=_F32)


def _dft_stage2(a4, m_hi, m_lo, fourier_w, layer, *, rows=8):
    b, h, l, zw = a4.shape
    w = FOURIER_WIDTH
    const = lambda bi, j: (0, 0)
    out = pl.pallas_call(
        functools.partial(_dft_stage2_kernel, rows=rows),
        grid=(b, h // rows),
        in_specs=[pl.BlockSpec((1, rows, l, zw), lambda bi, j: (bi, j, 0, 0)),
                  pl.BlockSpec(m_hi.shape, const), pl.BlockSpec(m_lo.shape, const),
                  pl.BlockSpec((None,) + fourier_w.shape[1:], lambda bi, j: (layer, 0, 0))],
        out_specs=pl.BlockSpec((1, l, rows, w), lambda bi, j: (bi, 0, j, 0)),
        out_shape=jax.ShapeDtypeStruct((b, l, h, w), _F32),
        compiler_params=_params("parallel", "parallel"),
        name="dft_stage2",
    )(a4, m_hi, m_lo, fourier_w)
    return out.reshape(b, l * h, w)


def _bias_feature_tables(tile):
    pos = np.arange(tile)
    hi, lo = (pos // LANES) * LANES, pos % LANES
    qf = np.zeros((tile, LANES), np.float32)
    kf = np.zeros((tile, LANES), np.float32)
    rest = LOG2_E
    for p in range(3):
        part = float(np.float32(rest).astype(jnp.bfloat16).astype(np.float32))
        rest -= part
        qf[:, 2 * p], qf[:, 2 * p + 1] = hi, lo
        kf[:, 2 * p], kf[:, 2 * p + 1] = part, part
        qf[:, 6 + 2 * p], qf[:, 7 + 2 * p] = -part, -part
        kf[:, 6 + 2 * p], kf[:, 7 + 2 * p] = hi, lo
    return jnp.asarray(qf, _BF16), jnp.asarray(np.stack([-kf, 0.0 * kf, kf]), _BF16)


def _attn_kernel(lam_init_ref, lam_ref, gain_ref, qft_ref, kf_ref, q_ref, k_ref, vt_ref, o_ref,
                 w_ref, acc_ref, p_ref, knorm_ref, *, slope, reach, q_tiles):
    t = q_ref.shape[1] // q_tiles
    n = k_ref.shape[1] // t
    step = pl.program_id(1)
    c = slope * LOG2_E
    nt = (((1,), (1,)), ((), ()))
    half_rows = (lax.broadcasted_iota(jnp.int32, (8, LANES), 1) // HEAD_DIM
                 == lax.broadcasted_iota(jnp.int32, (8, LANES), 0)).astype(_F32)

    def half_norms(x):
        sq = jnp.square(x.astype(_F32))
        return jnp.sqrt(lax.dot_general(half_rows, sq, nt, preferred_element_type=_F32)[0:2])

    @pl.when(step == 0)
    def _():
        knorm_ref[0] = jnp.max(half_norms(k_ref[0]))

    def explicit_bias(tile_delta):
        dist = jnp.abs(lax.broadcasted_iota(jnp.int32, (t, t), 0) - lax.broadcasted_iota(jnp.int32, (t, t), 1)
                       + tile_delta * t)
        bias = dist.astype(_F32) * (-c)
        return jnp.concatenate([bias, bias], axis=1)

    def pv(tile_idx, p_buf):
        k0 = pl.multiple_of(tile_idx * t, t)
        return jnp.dot(vt_ref[0, :, pl.ds(k0, t)], p_buf[...], preferred_element_type=_F32)

    def query_tile(u):
        qi = step * q_tiles + u
        rows = slice(u * t, (u + 1) * t)
        w_u, acc_u, p_a, p_b = w_ref.at[u], acc_ref.at[u], p_ref.at[2 * u], p_ref.at[2 * u + 1]

        qh = q_ref[0, rows, :]
        q_t = qh.astype(_F32).T
        row = lax.broadcasted_iota(jnp.int32, q_t.shape, 0)
        zero = jnp.zeros_like(q_t)
        halves = jnp.concatenate([jnp.where(row < HEAD_DIM, q_t, zero), jnp.where(row >= HEAD_DIM, q_t, zero)],
                                 axis=1)
        qf_t = qft_ref[...].astype(_F32) * slope
        w_u[...] = jnp.concatenate([halves, jnp.concatenate([qf_t, qf_t], axis=1)], axis=0).astype(_BF16)
        qn = [jnp.sqrt(jnp.sum(jnp.square(q_t[j * HEAD_DIM:(j + 1) * HEAD_DIM]), axis=0, keepdims=True))
              for j in range(2)]
        frame = jnp.concatenate(qn, axis=1) * (knorm_ref[0] * FRAME_SLACK) + 0.5

        def qk(tile_idx, side):
            k0 = pl.multiple_of(tile_idx * t, t)
            lhs = jnp.concatenate([k_ref[0, pl.ds(k0, t), :], kf_ref[side]], axis=1)
            return jnp.dot(lhs, w_u[...], preferred_element_type=_F32)

        def finish(o_all):
            lam_init = lam_init_ref[0]
            lp = lam_ref[...]
            lam = (jnp.exp(jnp.sum(lp[0:1] * lp[1:2], axis=-1, keepdims=True))
                   - jnp.exp(jnp.sum(lp[2:3] * lp[3:4], axis=-1, keepdims=True)) + lam_init)
            o = (o_all[:, :t] - lam * o_all[:, t:]).T
            o = _rms_scale(o, gain_ref[...]) * (1.0 - lam_init)
            o_ref[0, rows, :] = o.astype(o_ref.dtype)

        if 2 * reach + 1 >= n:
            others = [j + (qi <= j).astype(jnp.int32) for j in range(n - 1)]
            visits = [(qi, 1, None)] + [(i, jnp.where(i < qi, 0, 2), jnp.abs(i - qi).astype(_F32) * (-c * t))
                                        for i in others]
        else:
            visits = [(qi, 1, None)]
            for d in [sign * dist for dist in range(1, reach + 1) for sign in (-1, 1)]:
                i = qi + d
                inside = jnp.logical_and(i >= 0, i < n)
                visits.append((jnp.clip(i, 0, n - 1), 0 if d < 0 else 2,
                               jnp.where(inside, -c * t * abs(d), -NO_TILE)))

        acc_u[...] = jnp.zeros(acc_u.shape, _F32)
        bufs = (p_a, p_b)
        l = jnp.zeros((1, 2 * t), _F32)
        for v, (i, side, off) in enumerate(visits):
            if off is None:
                p = jnp.exp2(qk(i, side) + explicit_bias(0) - frame)
            else:
                p = jnp.exp2(qk(i, side) - (frame - off))
            l = l + jnp.sum(p, axis=0, keepdims=True)
            bufs[v % 2][...] = p.astype(_BF16)
            if v > 0:
                acc_u[...] += pv(visits[v - 1][0], bufs[(v - 1) % 2])
        last = len(visits) - 1
        o_all = (acc_u[...] + pv(visits[last][0], bufs[last % 2])) / l
        finish(o_all)
        unusable = jnp.logical_or(jnp.logical_not(jnp.sum(o_all * 0.0) == 0.0), jnp.min(l) < MIN_DENOMINATOR)

        def recompute():
            acc_u[...] = jnp.zeros(acc_u.shape, _F32)

            def safe_body(i, carry):
                m, l_run = carry
                sc = qk(i, 1) + explicit_bias(i - qi)
                m_new = jnp.maximum(m, jnp.max(sc, axis=0, keepdims=True))
                alpha = jnp.exp2(m - m_new)
                p = jnp.exp2(sc - m_new)
                p_a[...] = p.astype(_BF16)
                acc_u[...] = alpha * acc_u[...] + pv(i, p_a)
                return m_new, alpha * l_run + jnp.sum(p, axis=0, keepdims=True)

            init = (jnp.full((1, 2 * t), -jnp.inf, _F32), jnp.zeros((1, 2 * t), _F32))
            _, l_safe = lax.fori_loop(0, n, safe_body, init)
            finish(acc_u[...] / l_safe)

        return unusable, recompute

    for unusable, recompute in [query_tile(u) for u in range(q_tiles)]:
        pl.when(unusable)(recompute)


def _diff_attention_head(q, k, vt, head, slope, lam_init, lam_params, head_gain, *, tile=512):
    b, s, _ = q.shape
    n = s // tile
    qf, kf = _bias_feature_tables(tile)
    qf_t = qf.T
    reach = int(math.floor((FLUSH_EXP / (slope * LOG2_E) - 1) / tile)) + 1
    q_tiles = 2 if 2 * reach + 1 >= n else 4
    assert n >= 2 and n % q_tiles == 0
    smem = pl.BlockSpec(memory_space=pltpu.SMEM)
    const = lambda shape: pl.BlockSpec(shape, lambda bi, qi: (0,) * len(shape))
    return pl.pallas_call(
        functools.partial(_attn_kernel, slope=slope, reach=reach, q_tiles=q_tiles),
        grid=(b, n // q_tiles),
        in_specs=[smem, const(lam_params.shape),
                  pl.BlockSpec((1, V_DIM), lambda bi, qi: (0, head)),
                  const(qf_t.shape), const(kf.shape),
                  pl.BlockSpec((1, q_tiles * tile, V_DIM), lambda bi, qi: (bi, qi, head)),
                  pl.BlockSpec((1, s, V_DIM), lambda bi, qi: (bi, 0, head)),
                  pl.BlockSpec((1, V_DIM, s), lambda bi, qi: (bi, head, 0))],
        out_specs=pl.BlockSpec((1, q_tiles * tile, V_DIM), lambda bi, qi: (bi, qi, 0)),
        out_shape=jax.ShapeDtypeStruct((b, s, V_DIM), _BF16),
        scratch_shapes=[pltpu.VMEM((q_tiles, 2 * LANES, 2 * tile), _BF16),
                        pltpu.VMEM((q_tiles, V_DIM, 2 * tile), _F32),
                        pltpu.VMEM((2 * q_tiles, tile, 2 * tile), _BF16),
                        pltpu.SMEM((1,), _F32)],
        compiler_params=_params("arbitrary", "arbitrary"),
        name=f"diff_attn_h{head}",
    )(lam_init, lam_params, head_gain.reshape(1, ATTN_WIDTH), qf_t, kf, q, k, vt)


def _mix_out_ffn_kernel(x_ref, a_ref, a_prev_ref, a_next_ref, pw_ref, ps_ref, yf_ref, *rest, ff_chunk, final):
    head_refs, rest = rest[:ATTN_HEADS], rest[ATTN_HEADS:]
    if final:
        wo_ref, g_ref, wg_ref, wu_ref, wd_ref, gf_ref, o_ref, buf_ref = rest
    else:
        wo_ref, g_ref, wg_ref, wu_ref, wd_ref, o_ref, buf_ref = rest
    i = pl.program_id(1)
    tm = a_ref.shape[1]
    s_len = tm * pl.num_programs(1)
    halo = POOL_HALO
    buf_ref[0:halo] = jnp.where(i > 0, a_prev_ref[0], 0.0)
    buf_ref[halo:halo + tm] = a_ref[0]
    buf_ref[halo + tm:] = jnp.where(i < pl.num_programs(1) - 1, a_next_ref[0], 0.0)
    lane = lax.broadcasted_iota(jnp.int32, (1, LANES), 1)
    pick = lambda v0, v1: jnp.where(lane >= GROUP_DIM, v1, v0)

    def pooled(lanes, windows):
        shifted = lambda d: buf_ref[halo + d:halo + d + tm, lanes]
        a = shifted(0)
        sums, total, lo, hi = [], a, 0, 0
        for w in windows:
            left, right = w // 2, w - 1 - w // 2
            for d in list(range(-left, lo)) + list(range(hi + 1, right + 1)):
                total = total + shifted(d)
            lo, hi = -left, right
            sums.append(total)
        win = pick(*sums)
        mixed = win * pick(*[1.0 / w for w in windows]) - a

        def edge(rows):
            t = i * tm + rows.start + lax.broadcasted_iota(jnp.int32, (halo, LANES), 0)
            left, right = pick(*[w // 2 for w in windows]), pick(*[w - 1 - w // 2 for w in windows])
            cnt = jnp.minimum(t + right + 1, s_len) - jnp.maximum(t - left, 0)
            return win[rows] / cnt.astype(_F32) - a[rows]

        return jnp.concatenate([edge(slice(0, halo)), mixed[halo:tm - halo], edge(slice(tm - halo, tm))], axis=0)

    mixed = jnp.concatenate([pooled(slice(0, LANES), POOL_WINDOWS[:2]),
                             pooled(slice(LANES, 2 * LANES), POOL_WINDOWS[2:])], axis=1)
    y_pool = jnp.dot(mixed.astype(_BF16), pw_ref[...], preferred_element_type=_F32) * ps_ref[...]

    rest = jnp.concatenate([yf_ref[0].astype(_BF16)] + [r[0] for r in head_refs], axis=1)
    y = jnp.dot(rest, wo_ref[POOL_WIDTH:, :], preferred_element_type=_F32)
    y = y + jnp.dot(y_pool.astype(_BF16), wo_ref[:POOL_WIDTH, :], preferred_element_type=_F32)
    x = _swiglu_half_step(x_ref[0] + y, g_ref, wg_ref, wu_ref, wd_ref, ff_chunk)
    o_ref[0] = _rms_scale(x, gf_ref[...]) if final else x


def _mix_out_ffn(x, a, pool_bd, pool_scale, y_four, y_heads, w_out, g, wg, wu, wd, layer, final_g=None,
                 *, tm=512, ff_chunk=256):
    b, s, d = x.shape
    d_ff = wg.shape[2]
    hb = tm // POOL_HALO
    n_halo = s // POOL_HALO
    row = lambda bi, i: (bi, i, 0)
    const = lambda bi, i: (0, 0)
    resident = lambda shape: pl.BlockSpec((None,) + shape, lambda bi, i: (layer, 0, 0),
                                          pipeline_mode=pl.Buffered(1))
    in_specs = ([pl.BlockSpec((1, tm, d), row),
                 pl.BlockSpec((1, tm, POOL_WIDTH), row),
                 pl.BlockSpec((1, POOL_HALO, POOL_WIDTH), lambda bi, i: (bi, jnp.maximum(i * hb - 1, 0), 0)),
                 pl.BlockSpec((1, POOL_HALO, POOL_WIDTH),
                              lambda bi, i: (bi, jnp.minimum((i + 1) * hb, n_halo - 1), 0)),
                 pl.BlockSpec(pool_bd.shape, const), pl.BlockSpec((1, POOL_WIDTH), const),
                 pl.BlockSpec((1, tm, FOURIER_WIDTH), row)]
                + [pl.BlockSpec((1, tm, V_DIM), row)] * ATTN_HEADS
                + [resident((d, d)), pl.BlockSpec((1, d), const),
                   resident((d, d_ff)), resident((d, d_ff)), resident((d_ff, d))])
    args = [x, a, a, a, pool_bd, pool_scale.reshape(1, POOL_WIDTH), y_four, *y_heads, w_out,
            g.reshape(1, d), wg, wu, wd]
    if final_g is not None:
        in_specs.append(pl.BlockSpec((1, d), const))
        args.append(final_g.reshape(1, d))
    return pl.pallas_call(
        functools.partial(_mix_out_ffn_kernel, ff_chunk=ff_chunk, final=final_g is not None),
        grid=(b, s // tm),
        in_specs=in_specs,
        out_specs=pl.BlockSpec((1, tm, d), row),
        out_shape=jax.ShapeDtypeStruct((b, s, d), _F32),
        scratch_shapes=[pltpu.VMEM((tm + 2 * POOL_HALO, POOL_WIDTH), _F32)],
        compiler_params=_params("parallel", "parallel"),
        name="mix_out_ffn",
    )(*args)


def kernel(x, ffn1_norm, ffn1_w_gate, ffn1_w_up, ffn1_w_down, mix_norm, w_in, pool_w, pool_scale, fourier_w, lam_q1, lam_k1, lam_q2, lam_k2, attn_head_norm, w_out, ffn2_norm, ffn2_w_gate, ffn2_w_up, ffn2_w_down, final_norm):
    b, s, d = x.shape
    depth = w_in.shape[0]
    bf = lambda w: w.astype(_BF16)
    (chan_hi, chan_lo), (m1_hi, m1_lo), (m2_hi, m2_lo), tw_cos, tw_sin = _dft_tables(s)
    alibi = [2.0 ** (-8.0 * (i + 1) / ATTN_HEADS) for i in range(ATTN_HEADS)]
    assert all(math.frexp(v)[0] == 0.5 for v in alibi), "position features rely on power-of-two slopes"
    assert POOL_WIDTH + FOURIER_WIDTH == ATTN_WIDTH
    assert all(w & (w - 1) == 0 and w <= 2 * POOL_HALO for w in POOL_WINDOWS) and POOL_WIDTH == 2 * LANES
    ffn1 = (bf(ffn1_w_gate), bf(ffn1_w_up), bf(ffn1_w_down))
    ffn2 = (bf(ffn2_w_gate), bf(ffn2_w_up), bf(ffn2_w_down))
    w_in_b, w_out_b, fourier_b = bf(w_in), bf(w_out), bf(fourier_w)
    w_vt_b = bf(jnp.swapaxes(w_in[:, :, 3 * ATTN_WIDTH:], 1, 2))

    for l in range(depth):
        x, a, z, q, k, vt = _ffn_mix_in(x, ffn1_norm[l], *ffn1, mix_norm[l], w_in_b, w_vt_b, l, chan_hi, chan_lo)
        y_four = _dft_stage2(_dft_stage1(z, m1_hi, m1_lo, tw_cos, tw_sin), m2_hi, m2_lo, fourier_b, l)
        lam_init = jnp.full((1,), 0.8 - 0.6 * math.exp(-0.3 * l), _F32)
        lam_params = jnp.stack([lam_q1[l], lam_k1[l], lam_q2[l], lam_k2[l]]).astype(_F32)
        y_heads = [_diff_attention_head(q, k, vt, h, alibi[h], lam_init, lam_params, attn_head_norm[l])
                   for h in range(ATTN_HEADS)]
        pool_bd = jax.scipy.linalg.block_diag(*[pool_w[l, g] for g in range(len(POOL_WINDOWS))])
        x = _mix_out_ffn(x, a, bf(pool_bd), pool_scale[l], y_four, y_heads, w_out_b, ffn2_norm[l], *ffn2, l,
                         final_norm if l == depth - 1 else None)
    return x
```
